```python
import jax, jax.numpy as jnp
from jax import lax
import numpy as np

D_MODEL = 1024
BATCH = 8
SEQ = 2048
DEPTH = 2

N_MIXERS = 2
RMS_EPS = 1e-6
LN_EPS = 1e-5
CHUNK = 128
A_WIDTH = 2 * D_MODEL
A_GROUPS = 8
A_GROUP_DIM = A_WIDTH // A_GROUPS
B_WIDTH = 3 * D_MODEL // 2
B_HEADS = 12
B_HEAD_DIM = B_WIDTH // B_HEADS
CONV_WIDTH = 4
RG_C = 8.0

N_A_LAYERS = (DEPTH + 1) // 2
N_B_LAYERS = DEPTH // 2

kernel_name = "hybrid_sgu_rglru_trunk"


def rms_norm(x, w):
    xf = x.astype(jnp.float32)
    y = xf * lax.rsqrt(jnp.mean(xf * xf, axis=-1, keepdims=True) + RMS_EPS)
    return (y * w.astype(jnp.float32)).astype(x.dtype)


def layer_norm(x, w, b):
    xf = x.astype(jnp.float32)
    mu = jnp.mean(xf, axis=-1, keepdims=True)
    var = jnp.mean(jnp.square(xf - mu), axis=-1, keepdims=True)
    y = (xf - mu) * lax.rsqrt(var + LN_EPS)
    return (y * w.astype(jnp.float32) + b.astype(jnp.float32)).astype(x.dtype)


def chunked_sgu_mixer(h, w_in, ln_w, ln_b, w_s, b_s, w_out):
    B, S, _ = h.shape
    z = h @ w_in
    u, v, g = jnp.split(z, 3, axis=-1)
    u = jax.nn.gelu(u)
    v = layer_norm(jax.nn.gelu(v), ln_w, ln_b)
    v = v.reshape(B, S // CHUNK, CHUNK, A_GROUPS, A_GROUP_DIM)
    causal = jnp.tril(jnp.ones((CHUNK, CHUNK), dtype=w_s.dtype))
    w_causal = w_s * causal[None]
    s = jnp.einsum('gts,bnsgc->bntgc', w_causal, v) + b_s.T[None, None, :, :, None]
    y = u * s.reshape(B, S, A_WIDTH) * jax.nn.silu(g)
    return y @ w_out


def _linear_combine(left, right):
    a_l, b_l = left
    a_r, b_r = right
    return a_l * a_r, a_r * b_l + b_r


def rglru_mixer(h, w_in, conv_w, conv_b, gate_a_w, gate_a_b, gate_x_w, gate_x_b, lam, w_out):
    B, S, _ = h.shape
    xb, g = jnp.split(h @ w_in, 2, axis=-1)
    xp = jnp.pad(xb, ((0, 0), (CONV_WIDTH - 1, 0), (0, 0)))
    xc = conv_b + conv_w[CONV_WIDTH - 1] * xp[:, CONV_WIDTH - 1:CONV_WIDTH - 1 + S]
    for k in range(CONV_WIDTH - 1):
        xc = xc + conv_w[k] * xp[:, k:k + S]
    xh = xc.reshape(B, S, B_HEADS, B_HEAD_DIM)
    r = jax.nn.sigmoid(jnp.einsum('bshi,hij->bshj', xh, gate_a_w).reshape(B, S, B_WIDTH) + gate_a_b)
    i = jax.nn.sigmoid(jnp.einsum('bshi,hij->bshj', xh, gate_x_w).reshape(B, S, B_WIDTH) + gate_x_b)
    log_a = -RG_C * r.astype(jnp.float32) * jax.nn.softplus(-lam.astype(jnp.float32))
    a = jnp.exp(log_a)
    mult = jnp.sqrt(-jnp.expm1(2.0 * log_a))
    bterm = mult * (i * xc).astype(jnp.float32)
    _, hseq = lax.associative_scan(_linear_combine, (a, bterm), axis=1)
    y = hseq.astype(h.dtype) * jax.nn.silu(g)
    return y @ w_out


def setup_inputs(seed: int = 0) -> dict:
    key = jax.random.key(seed)
    ks = jax.random.split(key, 20)
    f32 = jnp.float32
    x = jax.random.normal(ks[0], (BATCH, SEQ, D_MODEL), f32)
    norm_w = 1.0 + 0.05 * jax.random.normal(ks[1], (DEPTH, D_MODEL), f32)
    a_w_in = jax.random.normal(ks[2], (N_A_LAYERS, D_MODEL, 3 * A_WIDTH), f32) * D_MODEL ** -0.5
    a_ln_w = 1.0 + 0.05 * jax.random.normal(ks[3], (N_A_LAYERS, A_WIDTH), f32)
    a_ln_b = 0.02 * jax.random.normal(ks[4], (N_A_LAYERS, A_WIDTH), f32)
    a_w_s = jax.random.normal(ks[5], (N_A_LAYERS, A_GROUPS, CHUNK, CHUNK), f32) * CHUNK ** -0.5
    a_b_s = 1.0 + 0.05 * jax.random.normal(ks[6], (N_A_LAYERS, A_GROUPS, CHUNK), f32)
    a_w_out = jax.random.normal(ks[7], (N_A_LAYERS, A_WIDTH, D_MODEL), f32) * A_WIDTH ** -0.5
    b_w_in = jax.random.normal(ks[8], (N_B_LAYERS, D_MODEL, 2 * B_WIDTH), f32) * D_MODEL ** -0.5
    b_conv_w = jax.random.normal(ks[9], (N_B_LAYERS, CONV_WIDTH, B_WIDTH), f32) * CONV_WIDTH ** -0.5
    b_conv_b = 0.02 * jax.random.normal(ks[10], (N_B_LAYERS, B_WIDTH), f32)
    b_gate_a_w = jax.random.normal(ks[11], (N_B_LAYERS, B_HEADS, B_HEAD_DIM, B_HEAD_DIM), f32) * B_HEAD_DIM ** -0.5
    b_gate_a_b = 0.02 * jax.random.normal(ks[12], (N_B_LAYERS, B_WIDTH), f32)
    b_gate_x_w = jax.random.normal(ks[13], (N_B_LAYERS, B_HEADS, B_HEAD_DIM, B_HEAD_DIM), f32) * B_HEAD_DIM ** -0.5
    b_gate_x_b = 0.02 * jax.random.normal(ks[14], (N_B_LAYERS, B_WIDTH), f32)
    a_c = jax.random.uniform(ks[15], (N_B_LAYERS, B_WIDTH), f32, minval=0.9, maxval=0.999)
    a0 = a_c ** (1.0 / RG_C)
    b_lambda = jnp.log(a0) - jnp.log1p(-a0)
    b_w_out = jax.random.normal(ks[16], (N_B_LAYERS, B_WIDTH, D_MODEL), f32) * B_WIDTH ** -0.5
    norm_f_w = 1.0 + 0.05 * jax.random.normal(ks[17], (D_MODEL,), f32)
    return {
        "x": x, "norm_w": norm_w,
        "a_w_in": a_w_in, "a_ln_w": a_ln_w, "a_ln_b": a_ln_b,
        "a_w_s": a_w_s, "a_b_s": a_b_s, "a_w_out": a_w_out,
        "b_w_in": b_w_in, "b_conv_w": b_conv_w, "b_conv_b": b_conv_b,
        "b_gate_a_w": b_gate_a_w, "b_gate_a_b": b_gate_a_b,
        "b_gate_x_w": b_gate_x_w, "b_gate_x_b": b_gate_x_b,
        "b_lambda": b_lambda, "b_w_out": b_w_out,
        "norm_f_w": norm_f_w,
    }


def reference(x, norm_w, a_w_in, a_ln_w, a_ln_b, a_w_s, a_b_s, a_w_out,
              b_w_in, b_conv_w, b_conv_b, b_gate_a_w, b_gate_a_b,
              b_gate_x_w, b_gate_x_b, b_lambda, b_w_out, norm_f_w):
    for layer in range(DEPTH):
        h = rms_norm(x, norm_w[layer])
        j = layer // N_MIXERS
        if layer % N_MIXERS == 0:
            y = chunked_sgu_mixer(h, a_w_in[j], a_ln_w[j], a_ln_b[j], a_w_s[j], a_b_s[j], a_w_out[j])
        else:
            y = rglru_mixer(h, b_w_in[j], b_conv_w[j], b_conv_b[j], b_gate_a_w[j], b_gate_a_b[j],
                            b_gate_x_w[j], b_gate_x_b[j], b_lambda[j], b_w_out[j])
        x = x + y
    return rms_norm(x, norm_f_w)
```

```python
import functools
import math

import jax
import jax.numpy as jnp
from jax import lax
from jax.experimental import pallas as pl
from jax.experimental.pallas import tpu as pltpu

D_MODEL = 1024
RMS_EPS = 1e-6
LN_EPS = 1e-5
CHUNK = 128
A_WIDTH = 2 * D_MODEL
A_GROUPS = 8
A_GROUP_DIM = A_WIDTH // A_GROUPS
B_WIDTH = 3 * D_MODEL // 2
B_HEADS = 12
B_HEAD_DIM = B_WIDTH // B_HEADS
CONV_WIDTH = 4
RG_C = 8.0

SUBLANES = 8
COLS = 256
B_PAIRS = B_WIDTH // COLS
TILE = 256
VMEM_LIMIT_BYTES = 56 * 1024 * 1024

_GELU_C = math.sqrt(2.0 / math.pi)


def _gelu(x):
    return 0.5 * x * (1.0 + jnp.tanh(_GELU_C * (x + 0.044715 * (x * x * x))))


def _sigmoid(x):
    return 0.5 * (1.0 + jnp.tanh(0.5 * x))


def _silu(x):
    h = 0.5 * x
    return h * (1.0 + jnp.tanh(h))


def _rms_norm(x, w):
    ms = jnp.mean(x * x, axis=-1, keepdims=True)
    return x * lax.rsqrt(ms + RMS_EPS) * w


def _dot(a, b):
    return jnp.dot(a, b, preferred_element_type=jnp.float32)


def _block_scan(a, b, carry):
    t, c = a.shape
    nb = t // SUBLANES
    a3 = a.reshape(nb, SUBLANES, c)
    b3 = b.reshape(nb, SUBLANES, c)
    sub = lax.broadcasted_iota(jnp.int32, (nb, SUBLANES, c), 1)
    for k in (1, 2, 4):
        keep = sub >= k
        a_sh = jnp.where(keep, pltpu.roll(a3, k, axis=1), 1.0)
        b_sh = jnp.where(keep, pltpu.roll(b3, k, axis=1), 0.0)
        b3 = a3 * b_sh + b3
        a3 = a3 * a_sh
    hs = []
    for i in range(nb):
        h = a3[i] * carry + b3[i]
        hs.append(h)
        carry = jnp.broadcast_to(h[SUBLANES - 1:SUBLANES, :], (SUBLANES, c))
    return jnp.concatenate(hs, axis=0), carry


def _trunk_kernel(x_ref, nw_ref, awin_ref, lnw_ref, lnb_ref, wbd_ref, bsx_ref, awout_ref,
                  bwin_ref, convw_ref, convb_ref, wga_ref, wgx_ref, gab_ref, gxb_ref, lam_ref, bwout_ref,
                  out_ref,
                  hb_ref, vbuf_ref, ybuf_ref, x1_ref, xbuf_ref, cc_ref, hc_ref):
    j = pl.program_id(1)
    t = x_ref.shape[0]

    @pl.when(j == 0)
    def _():
        cc_ref[...] = jnp.zeros_like(cc_ref)
        hc_ref[...] = jnp.zeros_like(hc_ref)

    x = x_ref[...]
    hb_ref[...] = _rms_norm(x, nw_ref[0:1, :]).astype(jnp.bfloat16)

    s1 = jnp.zeros((t, 1), jnp.float32)
    s2 = jnp.zeros((t, 1), jnp.float32)
    for g in range(A_GROUPS):
        c0 = A_WIDTH + g * COLS
        v = _gelu(_dot(hb_ref[...], awin_ref[:, c0:c0 + COLS]))
        vbuf_ref[:, g * COLS:(g + 1) * COLS] = v
        s1 = s1 + jnp.sum(v, axis=-1, keepdims=True)
        s2 = s2 + jnp.sum(v * v, axis=-1, keepdims=True)
    mu = s1 * (1.0 / A_WIDTH)
    var = s2 * (1.0 / A_WIDTH) - mu * mu
    rstd = lax.rsqrt(var + LN_EPS)
    mu_b = jnp.broadcast_to(mu, (t, COLS))
    rstd_b = jnp.broadcast_to(rstd, (t, COLS))

    for g in range(A_GROUPS):
        lo, hi = g * COLS, (g + 1) * COLS
        u = _gelu(_dot(hb_ref[...], awin_ref[:, lo:hi]))
        gate = _silu(_dot(hb_ref[...], awin_ref[:, 2 * A_WIDTH + lo:2 * A_WIDTH + hi]))
        vn = (vbuf_ref[:, lo:hi] - mu_b) * rstd_b * lnw_ref[:, lo:hi] + lnb_ref[:, lo:hi]
        s = _dot(wbd_ref[g], vn.astype(jnp.bfloat16)) + bsx_ref[:, lo:hi]
        ybuf_ref[:, lo:hi] = (u * s * gate).astype(jnp.bfloat16)

    for n in range(D_MODEL // COLS):
        lo, hi = n * COLS, (n + 1) * COLS
        x1_ref[:, lo:hi] = x_ref[:, lo:hi] + _dot(ybuf_ref[...], awout_ref[:, lo:hi])

    hb_ref[...] = _rms_norm(x1_ref[...], nw_ref[1:2, :]).astype(jnp.bfloat16)

    lam = lam_ref[...]
    neg_c_sp = -RG_C * (jnp.maximum(-lam, 0.0) + jnp.log1p(jnp.exp(-jnp.abs(lam))))

    for p in range(B_PAIRS):
        lo, hi = p * COLS, (p + 1) * COLS
        xb = _dot(hb_ref[...], bwin_ref[:, lo:hi])
        gb = _dot(hb_ref[...], bwin_ref[:, B_WIDTH + lo:B_WIDTH + hi])
        xbuf_ref[0:SUBLANES, :] = cc_ref[p]
        xbuf_ref[SUBLANES:SUBLANES + t, :] = xb
        cc_ref[p] = xb[t - SUBLANES:, :]
        xc = convb_ref[:, lo:hi] + convw_ref[3:4, lo:hi] * xb
        for k in range(CONV_WIDTH - 1):
            sh = CONV_WIDTH - 1 - k
            xc = xc + convw_ref[k:k + 1, lo:hi] * xbuf_ref[SUBLANES - sh:SUBLANES - sh + t, :]
        xcb = xc.astype(jnp.bfloat16)
        r = _sigmoid(_dot(xcb, wga_ref[p]) + gab_ref[:, lo:hi])
        i = _sigmoid(_dot(xcb, wgx_ref[p]) + gxb_ref[:, lo:hi])
        log_a = r * neg_c_sp[:, lo:hi]
        a = jnp.exp(log_a)
        om = 1.0 - a * a
        mult = om * lax.rsqrt(jnp.maximum(om, 1e-30))
        bterm = mult * (i * xc)
        h, carry = _block_scan(a, bterm, hc_ref[p])
        hc_ref[p] = carry
        ybuf_ref[:, lo:hi] = (h * _silu(gb)).astype(jnp.bfloat16)

    nf = nw_ref[2:3, :]
    x2 = []
    for n in range(D_MODEL // COLS):
        lo, hi = n * COLS, (n + 1) * COLS
        x2.append(x1_ref[:, lo:hi] + _dot(ybuf_ref[:, 0:B_WIDTH], bwout_ref[:, lo:hi]))
    x2 = jnp.concatenate(x2, axis=1)
    out_ref[...] = _rms_norm(x2, nf)


def _const_spec(shape):
    zeros = (0,) * len(shape)
    return pl.BlockSpec(shape, lambda b, j: zeros, pipeline_mode=pl.Buffered(1))


@jax.jit
def kernel(x, norm_w, a_w_in, a_ln_w, a_ln_b, a_w_s, a_b_s, a_w_out, b_w_in, b_conv_w, b_conv_b, b_gate_a_w, b_gate_a_b, b_gate_x_w, b_gate_x_b, b_lambda, b_w_out, norm_f_w):
    bsz, seq, d = x.shape
    assert d == D_MODEL and seq % TILE == 0 and TILE % CHUNK == 0
    assert norm_w.shape[0] == 2 and a_w_in.shape[0] == 1 and b_w_in.shape[0] == 1
    bf = jnp.bfloat16
    n_rep = TILE // CHUNK

    nw = jnp.concatenate([norm_w, norm_f_w[None, :]], axis=0)
    awin = a_w_in[0].astype(bf)
    causal = jnp.tril(jnp.ones((CHUNK, CHUNK), a_w_s.dtype))
    w_causal = a_w_s[0] * causal[None]
    eye = jnp.eye(n_rep, dtype=w_causal.dtype)
    wbd = jnp.einsum('mn,gts->gmtns', eye, w_causal).reshape(A_GROUPS, TILE, TILE).astype(bf)
    bsx = jnp.tile(jnp.repeat(a_b_s[0].T, A_GROUP_DIM, axis=1), (n_rep, 1))
    awout = a_w_out[0].astype(bf)
    bwin = b_w_in[0].astype(bf)

    def pair_block_diag(w):
        w = w.reshape(B_PAIRS, 2, B_HEAD_DIM, B_HEAD_DIM)
        z = jnp.zeros_like(w[:, 0])
        top = jnp.concatenate([w[:, 0], z], axis=2)
        bot = jnp.concatenate([z, w[:, 1]], axis=2)
        return jnp.concatenate([top, bot], axis=1).astype(bf)

    wga = pair_block_diag(b_gate_a_w[0])
    wgx = pair_block_diag(b_gate_x_w[0])
    bwout = b_w_out[0].astype(bf)
    row = lambda v: v.reshape(1, -1)

    operands = (x, nw, awin, row(a_ln_w[0]), row(a_ln_b[0]), wbd, bsx, awout,
                bwin, b_conv_w[0], row(b_conv_b[0]), wga, wgx, row(b_gate_a_b[0]), row(b_gate_x_b[0]),
                row(b_lambda[0]), bwout)
    in_specs = [pl.BlockSpec((None, TILE, D_MODEL), lambda b, j: (b, j, 0))]
    in_specs += [_const_spec(op.shape) for op in operands[1:]]

    f32 = jnp.float32
    scratch = [
        pltpu.VMEM((TILE, D_MODEL), bf),
        pltpu.VMEM((TILE, A_WIDTH), f32),
        pltpu.VMEM((TILE, A_WIDTH), bf),
        pltpu.VMEM((TILE, D_MODEL), f32),
        pltpu.VMEM((TILE + SUBLANES, COLS), f32),
        pltpu.VMEM((B_PAIRS, SUBLANES, COLS), f32),
        pltpu.VMEM((B_PAIRS, SUBLANES, COLS), f32),
    ]
    return pl.pallas_call(
        _trunk_kernel,
        grid=(bsz, seq // TILE),
        in_specs=in_specs,
        out_specs=pl.BlockSpec((None, TILE, D_MODEL), lambda b, j: (b, j, 0)),
        out_shape=jax.ShapeDtypeStruct(x.shape, x.dtype),
        scratch_shapes=scratch,
        compiler_params=pltpu.CompilerParams(
            dimension_semantics=("arbitrary", "arbitrary"),
            vmem_limit_bytes=VMEM_LIMIT_BYTES,
        ),
        name="hybrid_trunk",
    )(*operands)
```

```python
import math

import jax
import jax.numpy as jnp
from jax import lax
from jax.experimental import pallas as pl
from jax.experimental.pallas import tpu as pltpu

D_MODEL = 1024
RMS_EPS = 1e-6
LN_EPS = 1e-5
CHUNK = 128
A_WIDTH = 2 * D_MODEL
A_GROUPS = 8
A_GROUP_DIM = A_WIDTH // A_GROUPS
B_WIDTH = 3 * D_MODEL // 2
B_HEADS = 12
B_HEAD_DIM = B_WIDTH // B_HEADS
CONV_WIDTH = 4
RG_C = 8.0

SUBLANES = 8
COLS = 256
B_PAIRS = B_WIDTH // COLS
TILE = 256
VMEM_LIMIT_BYTES = 56 * 1024 * 1024

_GELU_C = math.sqrt(2.0 / math.pi)


def _gelu_x2(x):
    return x * (1.0 + jnp.tanh(x * (_GELU_C + (_GELU_C * 0.044715) * (x * x))))


def _one_plus_tanh(h):
    return 1.0 + jnp.tanh(h)


def _rms_norm(x, w):
    ms = jnp.mean(x * x, axis=-1, keepdims=True)
    return x * lax.rsqrt(ms + RMS_EPS) * w


def _dot(a, b):
    return jnp.dot(a, b, preferred_element_type=jnp.float32)


def _w(ref, rows, cols):
    return pltpu.bitcast(ref[rows, cols], jnp.bfloat16)


def _block_scan(a, b, carry):
    t, c = a.shape
    nb = t // SUBLANES
    a3 = a.reshape(nb, SUBLANES, c)
    b3 = b.reshape(nb, SUBLANES, c)
    sub = lax.broadcasted_iota(jnp.int32, (nb, SUBLANES, c), 1)
    for k in (1, 2, 4):
        keep = sub >= k
        a_sh = jnp.where(keep, pltpu.roll(a3, k, axis=1), 1.0)
        b_sh = jnp.where(keep, pltpu.roll(b3, k, axis=1), 0.0)
        b3 = a3 * b_sh + b3
        a3 = a3 * a_sh
    hs = []
    for i in range(nb):
        h = a3[i] * carry + b3[i]
        hs.append(h)
        carry = jnp.broadcast_to(h[SUBLANES - 1:SUBLANES, :], (SUBLANES, c))
    return jnp.concatenate(hs, axis=0), carry


def _trunk_kernel(x_ref, nw_ref, awin_ref, lnw_ref, lnb_ref, wbd_ref, bsx_ref, awout_ref,
                  bwin_ref, convw_ref, convb_ref, wga_ref, wgx_ref, gab_ref, gxb_ref, lam_ref, bwout_ref,
                  out_ref,
                  hb_ref, vbuf_ref, ybuf_ref, x1_ref, xbuf_ref, cc_ref, hc_ref):
    j = pl.program_id(1)
    t = x_ref.shape[0]
    all_rows = slice(None)

    @pl.when(j == 0)
    def _():
        cc_ref[...] = jnp.zeros_like(cc_ref)
        hc_ref[...] = jnp.zeros_like(hc_ref)

    x = x_ref[...]
    hb_ref[...] = _rms_norm(x, nw_ref[0:1, :]).astype(jnp.bfloat16)

    s1 = jnp.zeros((t, 1), jnp.float32)
    s2 = jnp.zeros((t, 1), jnp.float32)
    for g in range(A_GROUPS):
        c0 = A_WIDTH + g * COLS
        v2 = _gelu_x2(_dot(hb_ref[...], _w(awin_ref, all_rows, slice(c0, c0 + COLS))))
        vbuf_ref[:, g * COLS:(g + 1) * COLS] = v2
        s1 = s1 + jnp.sum(v2, axis=-1, keepdims=True)
        s2 = s2 + jnp.sum(v2 * v2, axis=-1, keepdims=True)
    mu = s1 * (1.0 / A_WIDTH)
    var = s2 * (1.0 / A_WIDTH) - mu * mu
    rstd = lax.rsqrt(var + 4.0 * LN_EPS)
    mu_b = jnp.broadcast_to(mu, (t, COLS))
    rstd_b = jnp.broadcast_to(rstd, (t, COLS))

    for g in range(A_GROUPS):
        lo, hi = g * COLS, (g + 1) * COLS
        u2 = _gelu_x2(_dot(hb_ref[...], _w(awin_ref, all_rows, slice(lo, hi))))
        gh = _dot(hb_ref[...], _w(awin_ref, all_rows, slice(2 * A_WIDTH + lo, 2 * A_WIDTH + hi)))
        gate = gh * _one_plus_tanh(gh)
        vn = (vbuf_ref[:, lo:hi] - mu_b) * rstd_b * lnw_ref[:, lo:hi] + lnb_ref[:, lo:hi]
        s_half = _dot(_w(wbd_ref, g, all_rows), vn.astype(jnp.bfloat16)) + bsx_ref[:, lo:hi]
        ybuf_ref[:, lo:hi] = (u2 * s_half * gate).astype(jnp.bfloat16)

    for n in range(D_MODEL // COLS):
        lo, hi = n * COLS, (n + 1) * COLS
        x1_ref[:, lo:hi] = x_ref[:, lo:hi] + _dot(ybuf_ref[...], _w(awout_ref, all_rows, slice(lo, hi)))

    hb_ref[...] = _rms_norm(x1_ref[...], nw_ref[1:2, :]).astype(jnp.bfloat16)

    lam = lam_ref[...]
    half_neg_c_sp = (-0.5 * RG_C) * (jnp.maximum(-lam, 0.0) + jnp.log1p(jnp.exp(-jnp.abs(lam))))

    for p in range(B_PAIRS):
        lo, hi = p * COLS, (p + 1) * COLS
        xb = _dot(hb_ref[...], _w(bwin_ref, all_rows, slice(lo, hi)))
        gbh = _dot(hb_ref[...], _w(bwin_ref, all_rows, slice(B_WIDTH + lo, B_WIDTH + hi)))
        xbuf_ref[0:SUBLANES, :] = cc_ref[p]
        xbuf_ref[SUBLANES:SUBLANES + t, :] = xb
        cc_ref[p] = xb[t - SUBLANES:, :]
        xc = convb_ref[:, lo:hi] + convw_ref[3:4, lo:hi] * xb
        for k in range(CONV_WIDTH - 1):
            sh = CONV_WIDTH - 1 - k
            xc = xc + convw_ref[k:k + 1, lo:hi] * xbuf_ref[SUBLANES - sh:SUBLANES - sh + t, :]
        xcb = xc.astype(jnp.bfloat16)
        r2 = _one_plus_tanh(_dot(xcb, _w(wga_ref, p, all_rows)) + gab_ref[:, lo:hi])
        i2 = _one_plus_tanh(_dot(xcb, _w(wgx_ref, p, all_rows)) + gxb_ref[:, lo:hi])
        a = jnp.exp(r2 * half_neg_c_sp[:, lo:hi])
        om = 1.0 - a * a
        mult = om * lax.rsqrt(jnp.maximum(om, 1e-30))
        h2, carry = _block_scan(a, mult * (i2 * xc), hc_ref[p])
        hc_ref[p] = carry
        ybuf_ref[:, lo:hi] = (h2 * (gbh * _one_plus_tanh(gbh))).astype(jnp.bfloat16)

    nf = nw_ref[2:3, :]
    x2 = []
    for n in range(D_MODEL // COLS):
        lo, hi = n * COLS, (n + 1) * COLS
        x2.append(x1_ref[:, lo:hi] + _dot(ybuf_ref[:, 0:B_WIDTH], _w(bwout_ref, all_rows, slice(lo, hi))))
    x2 = jnp.concatenate(x2, axis=1)
    out_ref[...] = _rms_norm(x2, nf)


def _const_spec(shape):
    zeros = (0,) * len(shape)
    return pl.BlockSpec(shape, lambda b, j: zeros, pipeline_mode=pl.Buffered(1))


def _pack_row_pairs(w):
    w = w.astype(jnp.bfloat16)
    *lead, k, n = w.shape
    w = jnp.swapaxes(w.reshape(*lead, k // 2, 2, n), -1, -2)
    return lax.bitcast_convert_type(w, jnp.uint32)


@jax.jit
def kernel(x, norm_w, a_w_in, a_ln_w, a_ln_b, a_w_s, a_b_s, a_w_out, b_w_in, b_conv_w, b_conv_b, b_gate_a_w, b_gate_a_b, b_gate_x_w, b_gate_x_b, b_lambda, b_w_out, norm_f_w):
    bsz, seq, d = x.shape
    assert d == D_MODEL and seq % TILE == 0 and TILE % CHUNK == 0
    assert norm_w.shape[0] == 2 and a_w_in.shape[0] == 1 and b_w_in.shape[0] == 1
    n_rep = TILE // CHUNK

    nw = jnp.concatenate([norm_w, norm_f_w[None, :]], axis=0)
    a_scale = jnp.concatenate([jnp.ones((2 * A_WIDTH,), x.dtype), jnp.full((A_WIDTH,), 0.5, x.dtype)])
    awin = _pack_row_pairs(a_w_in[0] * a_scale[None, :])
    causal = jnp.tril(jnp.ones((CHUNK, CHUNK), a_w_s.dtype))
    w_causal = 0.5 * a_w_s[0] * causal[None]
    eye = jnp.eye(n_rep, dtype=w_causal.dtype)
    wbd = _pack_row_pairs(jnp.einsum('mn,gts->gmtns', eye, w_causal).reshape(A_GROUPS, TILE, TILE))
    bsx = jnp.tile(jnp.repeat(0.5 * a_b_s[0].T, A_GROUP_DIM, axis=1), (n_rep, 1))
    awout = _pack_row_pairs(a_w_out[0])
    b_scale = jnp.concatenate([jnp.ones((B_WIDTH,), x.dtype), jnp.full((B_WIDTH,), 0.5, x.dtype)])
    bwin = _pack_row_pairs(b_w_in[0] * b_scale[None, :])

    def pair_block_diag(w):
        w = w.reshape(B_PAIRS, 2, B_HEAD_DIM, B_HEAD_DIM)
        z = jnp.zeros_like(w[:, 0])
        top = jnp.concatenate([w[:, 0], z], axis=2)
        bot = jnp.concatenate([z, w[:, 1]], axis=2)
        return jnp.concatenate([top, bot], axis=1)

    wga = _pack_row_pairs(pair_block_diag(0.5 * b_gate_a_w[0]))
    wgx = _pack_row_pairs(pair_block_diag(0.5 * b_gate_x_w[0]))
    bwout = _pack_row_pairs(0.5 * b_w_out[0])
    row = lambda v: v.reshape(1, -1)

    operands = (x, nw, awin, row(a_ln_w[0]), row(a_ln_b[0]), wbd, bsx, awout,
                bwin, b_conv_w[0], row(b_conv_b[0]), wga, wgx,
                row(0.5 * b_gate_a_b[0]), row(0.5 * b_gate_x_b[0]), row(b_lambda[0]), bwout)
    in_specs = [pl.BlockSpec((None, TILE, D_MODEL), lambda b, j: (b, j, 0))]
    in_specs += [_const_spec(op.shape) for op in operands[1:]]

    f32, bf = jnp.float32, jnp.bfloat16
    scratch = [
        pltpu.VMEM((TILE, D_MODEL), bf),
        pltpu.VMEM((TILE, A_WIDTH), f32),
        pltpu.VMEM((TILE, A_WIDTH), bf),
        pltpu.VMEM((TILE, D_MODEL), f32),
        pltpu.VMEM((TILE + SUBLANES, COLS), f32),
        pltpu.VMEM((B_PAIRS, SUBLANES, COLS), f32),
        pltpu.VMEM((B_PAIRS, SUBLANES, COLS), f32),
    ]
    return pl.pallas_call(
        _trunk_kernel,
        grid=(bsz, seq // TILE),
        in_specs=in_specs,
        out_specs=pl.BlockSpec((None, TILE, D_MODEL), lambda b, j: (b, j, 0)),
        out_shape=jax.ShapeDtypeStruct(x.shape, x.dtype),
        scratch_shapes=scratch,
        compiler_params=pltpu.CompilerParams(
            dimension_semantics=("arbitrary", "arbitrary"),
            vmem_limit_bytes=VMEM_LIMIT_BYTES,
        ),
        name="hybrid_trunk",
    )(*operands)
```

```python
import math

import jax
import jax.numpy as jnp
from jax import lax
from jax.experimental import pallas as pl
from jax.experimental.pallas import tpu as pltpu

D_MODEL = 1024
RMS_EPS = 1e-6
LN_EPS = 1e-5
CHUNK = 128
A_WIDTH = 2 * D_MODEL
A_GROUPS = 8
A_GROUP_DIM = A_WIDTH // A_GROUPS
B_WIDTH = 3 * D_MODEL // 2
B_HEADS = 12
B_HEAD_DIM = B_WIDTH // B_HEADS
CONV_WIDTH = 4
RG_C = 8.0

SUBLANES = 8
LANES = 128
COLS = 256
B_PAIRS = B_WIDTH // COLS
TILE = 256
SEG = TILE // SUBLANES
PITCH = SEG + SUBLANES
HIST = (CONV_WIDTH - 1) * SUBLANES
VMEM_LIMIT_BYTES = 56 * 1024 * 1024

_GELU_C = math.sqrt(2.0 / math.pi)


def _gelu_x2(x):
    return x * (1.0 + jnp.tanh(x * (_GELU_C + (_GELU_C * 0.044715) * (x * x))))


def _one_plus_tanh(h):
    return 1.0 + jnp.tanh(h)


def _rms_norm(x, w):
    ms = jnp.mean(x * x, axis=-1, keepdims=True)
    return x * lax.rsqrt(ms + RMS_EPS) * w


def _dot(a, b):
    return jnp.dot(a, b, preferred_element_type=jnp.float32)


def _w(ref, rows, cols):
    return pltpu.bitcast(ref[rows, cols], jnp.bfloat16)


def _to_segment_order(buf_ref, x):
    n_slab = x.shape[1] // LANES
    for c in range(n_slab):
        for s in range(SUBLANES):
            buf_ref[c, s * PITCH:s * PITCH + SEG, :] = x[s * SEG:(s + 1) * SEG, c * LANES:(c + 1) * LANES]
    slabs = [jnp.concatenate([buf_ref[c, pl.ds(k, SUBLANES, stride=PITCH), :] for k in range(SEG)], axis=0)
             for c in range(n_slab)]
    return jnp.concatenate(slabs, axis=1)


def _from_segment_order(buf_ref, steps):
    n_slab = steps[0].shape[1] // LANES
    for c in range(n_slab):
        for k in range(SEG):
            buf_ref[c, pl.ds(k, SUBLANES, stride=PITCH), :] = steps[k][:, c * LANES:(c + 1) * LANES]
    slabs = [jnp.concatenate([buf_ref[c, s * PITCH:s * PITCH + SEG, :] for s in range(SUBLANES)], axis=0)
             for c in range(n_slab)]
    return jnp.concatenate(slabs, axis=1)


def _sublane_scan(p, h, sub):
    for k in (1, 2, 4):
        keep = sub >= k
        p_sh = jnp.where(keep, pltpu.roll(p, k, axis=0), 1.0)
        h_sh = jnp.where(keep, pltpu.roll(h, k, axis=0), 0.0)
        h = p * h_sh + h
        p = p * p_sh
    return p, h


def _trunk_kernel(x_ref, nw_ref, awin_ref, lnw_ref, lnb_ref, wbd_ref, bsx_ref, awout_ref,
                  bwin_ref, convw_ref, convb_ref, wga_ref, wgx_ref, gab_ref, gxb_ref, lam_ref, bwout_ref,
                  out_ref,
                  hb_ref, vbuf_ref, ybuf_ref, x1_ref, pin_ref, pout_ref, cc_ref, hc_ref):
    j = pl.program_id(1)
    t = x_ref.shape[0]
    all_rows = slice(None)

    @pl.when(j == 0)
    def _():
        cc_ref[...] = jnp.zeros_like(cc_ref)
        hc_ref[...] = jnp.zeros_like(hc_ref)

    x = x_ref[...]
    hb_ref[...] = _rms_norm(x, nw_ref[0:1, :]).astype(jnp.bfloat16)

    s1 = jnp.zeros((t, 1), jnp.float32)
    s2 = jnp.zeros((t, 1), jnp.float32)
    for g in range(A_GROUPS):
        c0 = A_WIDTH + g * COLS
        v2 = _gelu_x2(_dot(hb_ref[...], _w(awin_ref, all_rows, slice(c0, c0 + COLS))))
        vbuf_ref[:, g * COLS:(g + 1) * COLS] = v2
        s1 = s1 + jnp.sum(v2, axis=-1, keepdims=True)
        s2 = s2 + jnp.sum(v2 * v2, axis=-1, keepdims=True)
    mu = s1 * (1.0 / A_WIDTH)
    var = s2 * (1.0 / A_WIDTH) - mu * mu
    rstd = lax.rsqrt(var + 4.0 * LN_EPS)
    mu_b = jnp.broadcast_to(mu, (t, COLS))
    rstd_b = jnp.broadcast_to(rstd, (t, COLS))

    for g in range(A_GROUPS):
        lo, hi = g * COLS, (g + 1) * COLS
        u2 = _gelu_x2(_dot(hb_ref[...], _w(awin_ref, all_rows, slice(lo, hi))))
        gh = _dot(hb_ref[...], _w(awin_ref, all_rows, slice(2 * A_WIDTH + lo, 2 * A_WIDTH + hi)))
        gate = gh * _one_plus_tanh(gh)
        vn = (vbuf_ref[:, lo:hi] - mu_b) * rstd_b * lnw_ref[:, lo:hi] + lnb_ref[:, lo:hi]
        s_half = _dot(_w(wbd_ref, g, all_rows), vn.astype(jnp.bfloat16)) + bsx_ref[:, lo:hi]
        ybuf_ref[:, lo:hi] = (u2 * s_half * gate).astype(jnp.bfloat16)

    for n in range(D_MODEL // COLS):
        lo, hi = n * COLS, (n + 1) * COLS
        x1_ref[:, lo:hi] = x_ref[:, lo:hi] + _dot(ybuf_ref[...], _w(awout_ref, all_rows, slice(lo, hi)))

    hb_ref[...] = _rms_norm(x1_ref[...], nw_ref[1:2, :]).astype(jnp.bfloat16)

    lam = lam_ref[...]
    half_neg_c_sp = (-0.5 * RG_C) * (jnp.maximum(-lam, 0.0) + jnp.log1p(jnp.exp(-jnp.abs(lam))))
    sub = lax.broadcasted_iota(jnp.int32, (SUBLANES, COLS), 0)
    sub_hist = lax.broadcasted_iota(jnp.int32, (CONV_WIDTH - 1, SUBLANES, COLS), 1).reshape(HIST, COLS)

    for p in range(B_PAIRS):
        lo, hi = p * COLS, (p + 1) * COLS
        xb = _dot(hb_ref[...], _w(bwin_ref, all_rows, slice(lo, hi)))
        gbh = _dot(hb_ref[...], _w(bwin_ref, all_rows, slice(B_WIDTH + lo, B_WIDTH + hi)))
        xp = _to_segment_order(pin_ref, xb)
        tail = xp[t - HIST:, :]
        roll_rows = lambda z: pltpu.roll(z.reshape(CONV_WIDTH - 1, SUBLANES, COLS), 1, axis=1).reshape(HIST, COLS)
        head = jnp.where(sub_hist == 0, roll_rows(cc_ref[p]), roll_rows(tail))
        cc_ref[p] = tail
        ext = jnp.concatenate([head, xp], axis=0)
        xc = convb_ref[:, lo:hi] + convw_ref[3:4, lo:hi] * xp
        for k in range(CONV_WIDTH - 1):
            xc = xc + convw_ref[k:k + 1, lo:hi] * ext[k * SUBLANES:k * SUBLANES + t, :]
        xcb = xc.astype(jnp.bfloat16)
        r2 = _one_plus_tanh(_dot(xcb, _w(wga_ref, p, all_rows)) + gab_ref[:, lo:hi])
        i2 = _one_plus_tanh(_dot(xcb, _w(wgx_ref, p, all_rows)) + gxb_ref[:, lo:hi])
        a = jnp.exp(r2 * half_neg_c_sp[:, lo:hi])
        om = 1.0 - a * a
        mult = om * lax.rsqrt(jnp.maximum(om, 1e-30))
        b2 = mult * (i2 * xc)
        hs, ps = [], []
        h_loc = jnp.zeros((SUBLANES, COLS), jnp.float32)
        p_loc = jnp.ones((SUBLANES, COLS), jnp.float32)
        for k in range(SEG):
            a_k = a[k * SUBLANES:(k + 1) * SUBLANES, :]
            h_loc = a_k * h_loc + b2[k * SUBLANES:(k + 1) * SUBLANES, :]
            p_loc = a_k * p_loc
            hs.append(h_loc)
            ps.append(p_loc)
        carry = hc_ref[p]
        p_cum, h_cum = _sublane_scan(p_loc, h_loc, sub)
        seg_end = h_cum + p_cum * carry
        seg_in = jnp.where(sub == 0, carry, pltpu.roll(seg_end, 1, axis=0))
        hc_ref[p] = jnp.broadcast_to(seg_end[SUBLANES - 1:SUBLANES, :], (SUBLANES, COLS))
        h2 = _from_segment_order(pout_ref, [hs[k] + ps[k] * seg_in for k in range(SEG)])
        ybuf_ref[:, lo:hi] = (h2 * (gbh * _one_plus_tanh(gbh))).astype(jnp.bfloat16)

    nf = nw_ref[2:3, :]
    x2 = []
    for n in range(D_MODEL // COLS):
        lo, hi = n * COLS, (n + 1) * COLS
        x2.append(x1_ref[:, lo:hi] + _dot(ybuf_ref[:, 0:B_WIDTH], _w(bwout_ref, all_rows, slice(lo, hi))))
    x2 = jnp.concatenate(x2, axis=1)
    out_ref[...] = _rms_norm(x2, nf)


def _const_spec(shape):
    zeros = (0,) * len(shape)
    return pl.BlockSpec(shape, lambda b, j: zeros, pipeline_mode=pl.Buffered(1))


def _pack_row_pairs(w):
    bits = lax.bitcast_convert_type(w.astype(jnp.bfloat16), jnp.uint16).astype(jnp.uint32)
    return bits[..., 0::2, :] | (bits[..., 1::2, :] << 16)


@jax.jit
def kernel(x, norm_w, a_w_in, a_ln_w, a_ln_b, a_w_s, a_b_s, a_w_out, b_w_in, b_conv_w, b_conv_b, b_gate_a_w, b_gate_a_b, b_gate_x_w, b_gate_x_b, b_lambda, b_w_out, norm_f_w):
    bsz, seq, d = x.shape
    assert d == D_MODEL and seq % TILE == 0 and TILE % CHUNK == 0
    assert norm_w.shape[0] == 2 and a_w_in.shape[0] == 1 and b_w_in.shape[0] == 1
    n_rep = TILE // CHUNK

    nw = jnp.concatenate([norm_w, norm_f_w[None, :]], axis=0)
    a_scale = jnp.concatenate([jnp.ones((2 * A_WIDTH,), x.dtype), jnp.full((A_WIDTH,), 0.5, x.dtype)])
    awin = _pack_row_pairs(a_w_in[0] * a_scale[None, :])
    causal = jnp.tril(jnp.ones((CHUNK, CHUNK), a_w_s.dtype))
    w_causal = 0.5 * a_w_s[0] * causal[None]
    eye = jnp.eye(n_rep, dtype=w_causal.dtype)
    wbd = _pack_row_pairs(jnp.einsum('mn,gts->gmtns', eye, w_causal).reshape(A_GROUPS, TILE, TILE))
    bsx = jnp.tile(jnp.repeat(0.5 * a_b_s[0].T, A_GROUP_DIM, axis=1), (n_rep, 1))
    awout = _pack_row_pairs(a_w_out[0])
    b_scale = jnp.concatenate([jnp.ones((B_WIDTH,), x.dtype), jnp.full((B_WIDTH,), 0.5, x.dtype)])
    bwin = _pack_row_pairs(b_w_in[0] * b_scale[None, :])

    def pair_block_diag(w):
        w = w.reshape(B_PAIRS, 2, B_HEAD_DIM, B_HEAD_DIM)
        z = jnp.zeros_like(w[:, 0])
        top = jnp.concatenate([w[:, 0], z], axis=2)
        bot = jnp.concatenate([z, w[:, 1]], axis=2)
        return jnp.concatenate([top, bot], axis=1)

    wga = _pack_row_pairs(pair_block_diag(0.5 * b_gate_a_w[0]))
    wgx = _pack_row_pairs(pair_block_diag(0.5 * b_gate_x_w[0]))
    bwout = _pack_row_pairs(0.5 * b_w_out[0])
    row = lambda v: v.reshape(1, -1)

    operands = (x, nw, awin, row(a_ln_w[0]), row(a_ln_b[0]), wbd, bsx, awout,
                bwin, b_conv_w[0], row(b_conv_b[0]), wga, wgx,
                row(0.5 * b_gate_a_b[0]), row(0.5 * b_gate_x_b[0]), row(b_lambda[0]), bwout)
    in_specs = [pl.BlockSpec((None, TILE, D_MODEL), lambda b, j: (b, j, 0))]
    in_specs += [_const_spec(op.shape) for op in operands[1:]]

    f32, bf = jnp.float32, jnp.bfloat16
    scratch = [
        pltpu.VMEM((TILE, D_MODEL), bf),
        pltpu.VMEM((TILE, A_WIDTH), f32),
        pltpu.VMEM((TILE, A_WIDTH), bf),
        pltpu.VMEM((TILE, D_MODEL), f32),
        pltpu.VMEM((COLS // LANES, SUBLANES * PITCH, LANES), f32),
        pltpu.VMEM((COLS // LANES, SUBLANES * PITCH, LANES), f32),
        pltpu.VMEM((B_PAIRS, HIST, COLS), f32),
        pltpu.VMEM((B_PAIRS, SUBLANES, COLS), f32),
    ]
    return pl.pallas_call(
        _trunk_kernel,
        grid=(bsz, seq // TILE),
        in_specs=in_specs,
        out_specs=pl.BlockSpec((None, TILE, D_MODEL), lambda b, j: (b, j, 0)),
        out_shape=jax.ShapeDtypeStruct(x.shape, x.dtype),
        scratch_shapes=scratch,
        compiler_params=pltpu.CompilerParams(
            dimension_semantics=("arbitrary", "arbitrary"),
            vmem_limit_bytes=VMEM_LIMIT_BYTES,
        ),
        name="hybrid_trunk",
    )(*operands)
```

```python
import functools
import math

import jax
import jax.numpy as jnp
from jax import lax
from jax.experimental import pallas as pl
from jax.experimental.pallas import tpu as pltpu

D_MODEL = 1024
RMS_EPS = 1e-6
LN_EPS = 1e-5
CHUNK = 128
A_WIDTH = 2 * D_MODEL
A_GROUPS = 8
A_GROUP_DIM = A_WIDTH // A_GROUPS
B_WIDTH = 3 * D_MODEL // 2
B_HEADS = 12
B_HEAD_DIM = B_WIDTH // B_HEADS
CONV_WIDTH = 4
RG_C = 8.0

SUBLANES = 8
LANES = 128
COLS = 256
B_PAIRS = B_WIDTH // COLS
TILE = 256
SEG = TILE // SUBLANES
PITCH = SEG + SUBLANES
HIST = (CONV_WIDTH - 1) * SUBLANES
WCOLS = 1024
WROWS = 256
VMEM_LIMIT_BYTES = 56 * 1024 * 1024

_GELU_C = math.sqrt(2.0 / math.pi)


def _gelu_x2(x):
    return x * (1.0 + jnp.tanh(x * (_GELU_C + (_GELU_C * 0.044715) * (x * x))))


def _one_plus_tanh(h):
    return 1.0 + jnp.tanh(h)


def _rms_norm(x, w):
    ms = jnp.mean(x * x, axis=-1, keepdims=True)
    return x * lax.rsqrt(ms + RMS_EPS) * w


def _dot(a, b):
    return jnp.dot(a, b, preferred_element_type=jnp.float32)


def _wcols(ref, lo, hi):
    assert lo // WCOLS == (hi - 1) // WCOLS
    return ref[lo // WCOLS, :, lo % WCOLS:lo % WCOLS + (hi - lo)]


def _packed(ref, idx):
    return pltpu.bitcast(ref[idx], jnp.bfloat16)


def _convert_weight(w_hbm, dst_ref, scale, stage_ref, sem):
    n_slab, k, _ = dst_ref.shape
    n_rb = k // WROWS
    n = n_slab * n_rb

    def copy(q):
        rb, c = q // n_slab, q % n_slab
        src = w_hbm.at[0, pl.ds(rb * WROWS, WROWS), pl.ds(pl.multiple_of(c * WCOLS, WCOLS), WCOLS)]
        return pltpu.make_async_copy(src, stage_ref.at[q % 2], sem.at[q % 2])

    copy(0).start()

    def body(q, _):
        copy(q).wait()

        @pl.when(q + 1 < n)
        def _():
            copy(q + 1).start()

        rb, c = q // n_slab, q % n_slab
        factor = scale if isinstance(scale, float) else scale[pl.ds(c, 1), :]
        dst_ref[c, pl.ds(pl.multiple_of(rb * WROWS, WROWS), WROWS), :] = (stage_ref[q % 2] * factor).astype(jnp.bfloat16)
        return 0

    lax.fori_loop(0, n, body, 0)


def _to_segment_order(buf_ref, x):
    n_slab = x.shape[1] // LANES
    for c in range(n_slab):
        for s in range(SUBLANES):
            buf_ref[c, s * PITCH:s * PITCH + SEG, :] = x[s * SEG:(s + 1) * SEG, c * LANES:(c + 1) * LANES]
    slabs = [jnp.concatenate([buf_ref[c, pl.ds(k, SUBLANES, stride=PITCH), :] for k in range(SEG)], axis=0)
             for c in range(n_slab)]
    return jnp.concatenate(slabs, axis=1)


def _from_segment_order(buf_ref, steps):
    n_slab = steps[0].shape[1] // LANES
    for c in range(n_slab):
        for k in range(SEG):
            buf_ref[c, pl.ds(k, SUBLANES, stride=PITCH), :] = steps[k][:, c * LANES:(c + 1) * LANES]
    slabs = [jnp.concatenate([buf_ref[c, s * PITCH:s * PITCH + SEG, :] for s in range(SUBLANES)], axis=0)
             for c in range(n_slab)]
    return jnp.concatenate(slabs, axis=1)


def _sublane_scan(p, h, sub):
    for k in (1, 2, 4):
        keep = sub >= k
        p_sh = jnp.where(keep, pltpu.roll(p, k, axis=0), 1.0)
        h_sh = jnp.where(keep, pltpu.roll(h, k, axis=0), 0.0)
        h = p * h_sh + h
        p = p * p_sh
    return p, h


def _layer0(x_ref, nw_ref, awin_ref, lnw_ref, lnb_ref, wbd_ref, bsx_ref, awout_ref,
            hb_ref, vbuf_ref, ybuf_ref, x1_ref):
    t = x_ref.shape[0]
    hb_ref[...] = _rms_norm(x_ref[...], nw_ref[0:1, :]).astype(jnp.bfloat16)
    yield

    s1 = jnp.zeros((t, 1), jnp.float32)
    s2 = jnp.zeros((t, 1), jnp.float32)
    for g in range(A_GROUPS):
        lo, hi = g * COLS, (g + 1) * COLS
        v2 = _gelu_x2(_dot(hb_ref[...], _wcols(awin_ref, A_WIDTH + lo, A_WIDTH + hi)))
        vbuf_ref[:, lo:hi] = v2
        s1 = s1 + jnp.sum(v2, axis=-1, keepdims=True)
        s2 = s2 + jnp.sum(v2 * v2, axis=-1, keepdims=True)
        yield
    mu = s1 * (1.0 / A_WIDTH)
    var = s2 * (1.0 / A_WIDTH) - mu * mu
    rstd = lax.rsqrt(var + 4.0 * LN_EPS)
    mu_b = jnp.broadcast_to(mu, (t, COLS))
    rstd_b = jnp.broadcast_to(rstd, (t, COLS))

    for g in range(A_GROUPS):
        lo, hi = g * COLS, (g + 1) * COLS
        u2 = _gelu_x2(_dot(hb_ref[...], _wcols(awin_ref, lo, hi)))
        gh = _dot(hb_ref[...], _wcols(awin_ref, 2 * A_WIDTH + lo, 2 * A_WIDTH + hi))
        gate = gh * _one_plus_tanh(gh)
        vn = (vbuf_ref[:, lo:hi] - mu_b) * rstd_b * lnw_ref[:, lo:hi] + lnb_ref[:, lo:hi]
        s_half = _dot(_packed(wbd_ref, g), vn.astype(jnp.bfloat16)) + bsx_ref[:, lo:hi]
        ybuf_ref[:, lo:hi] = (u2 * s_half * gate).astype(jnp.bfloat16)
        yield

    for n in range(D_MODEL // COLS):
        lo, hi = n * COLS, (n + 1) * COLS
        x1_ref[:, lo:hi] = x_ref[:, lo:hi] + _dot(ybuf_ref[...], _wcols(awout_ref, lo, hi))
        yield


def _layer1(first_tile, hb_ref, x1_ref, nw_ref, bwin_ref, convw_ref, convb_ref, wga_ref, wgx_ref, gab_ref, gxb_ref,
            lam_ref, bwout_ref, out_ref, ybuf_ref, pin_ref, pout_ref, cc_ref, hc_ref):
    t = x1_ref.shape[0]
    lam = lam_ref[...]
    half_neg_c_sp = (-0.5 * RG_C) * (jnp.maximum(-lam, 0.0) + jnp.log1p(jnp.exp(-jnp.abs(lam))))
    sub = lax.broadcasted_iota(jnp.int32, (SUBLANES, COLS), 0)
    sub_hist = lax.broadcasted_iota(jnp.int32, (CONV_WIDTH - 1, SUBLANES, COLS), 1).reshape(HIST, COLS)
    roll_rows = lambda z: pltpu.roll(z.reshape(CONV_WIDTH - 1, SUBLANES, COLS), 1, axis=1).reshape(HIST, COLS)

    for p in range(B_PAIRS):
        lo, hi = p * COLS, (p + 1) * COLS
        xb = _dot(hb_ref[...], _wcols(bwin_ref, lo, hi))
        gbh = _dot(hb_ref[...], _wcols(bwin_ref, B_WIDTH + lo, B_WIDTH + hi))
        xp = _to_segment_order(pin_ref.at[p % 2], xb)
        tail = xp[t - HIST:, :]
        prev_tail = jnp.where(first_tile, 0.0, cc_ref[p])
        head = jnp.where(sub_hist == 0, roll_rows(prev_tail), roll_rows(tail))
        cc_ref[p] = tail
        ext = jnp.concatenate([head, xp], axis=0)
        xc = convb_ref[:, lo:hi] + convw_ref[3:4, lo:hi] * xp
        for k in range(CONV_WIDTH - 1):
            xc = xc + convw_ref[k:k + 1, lo:hi] * ext[k * SUBLANES:k * SUBLANES + t, :]
        xcb = xc.astype(jnp.bfloat16)
        yield
        r2 = _one_plus_tanh(_dot(xcb, _packed(wga_ref, p)) + gab_ref[:, lo:hi])
        i2 = _one_plus_tanh(_dot(xcb, _packed(wgx_ref, p)) + gxb_ref[:, lo:hi])
        a = jnp.exp(r2 * half_neg_c_sp[:, lo:hi])
        om = 1.0 - a * a
        mult = om * lax.rsqrt(jnp.maximum(om, 1e-30))
        b2 = mult * (i2 * xc)
        hs, ps = [], []
        h_loc = jnp.zeros((SUBLANES, COLS), jnp.float32)
        p_loc = jnp.ones((SUBLANES, COLS), jnp.float32)
        for k in range(SEG):
            a_k = a[k * SUBLANES:(k + 1) * SUBLANES, :]
            h_loc = a_k * h_loc + b2[k * SUBLANES:(k + 1) * SUBLANES, :]
            p_loc = a_k * p_loc
            hs.append(h_loc)
            ps.append(p_loc)
        carry = jnp.where(first_tile, 0.0, hc_ref[p])
        p_cum, h_cum = _sublane_scan(p_loc, h_loc, sub)
        seg_end = h_cum + p_cum * carry
        seg_in = jnp.where(sub == 0, carry, pltpu.roll(seg_end, 1, axis=0))
        hc_ref[p] = jnp.broadcast_to(seg_end[SUBLANES - 1:SUBLANES, :], (SUBLANES, COLS))
        h2 = _from_segment_order(pout_ref.at[p % 2], [hs[k] + ps[k] * seg_in for k in range(SEG)])
        ybuf_ref[:, lo:hi] = (h2 * (gbh * _one_plus_tanh(gbh))).astype(jnp.bfloat16)
        yield

    x2 = []
    for n in range(D_MODEL // COLS):
        lo, hi = n * COLS, (n + 1) * COLS
        x2.append(x1_ref[:, lo:hi] + _dot(ybuf_ref[...], _wcols(bwout_ref, lo, hi)))
        yield
    out_ref[...] = _rms_norm(jnp.concatenate(x2, axis=1), nw_ref[2:3, :])


def _interleave(*stage_generators):
    alive = list(stage_generators)
    while alive:
        for gen in list(alive):
            try:
                next(gen)
            except StopIteration:
                alive.remove(gen)


def _trunk_kernel(tiles_per_row,
                  x_ref, nw_ref, awin_hbm, awin_scale_ref, lnw_ref, lnb_ref, wbd_ref, bsx_ref, awout_hbm,
                  bwin_hbm, bwin_scale_ref, convw_ref, convb_ref, wga_ref, wgx_ref, gab_ref, gxb_ref, lam_ref,
                  bwout_hbm,
                  out_ref,
                  awin_ref, awout_ref, bwin_ref, bwout_ref, stage_ref, sem,
                  hba_ref, vbuf_ref, yba_ref, x1a_ref,
                  hbb_ref, x1b_ref, ybb_ref, pin_ref, pout_ref, cc_ref, hc_ref):
    i = pl.program_id(0)

    @pl.when(i == 0)
    def _():
        _convert_weight(awin_hbm, awin_ref, awin_scale_ref, stage_ref, sem)
        _convert_weight(awout_hbm, awout_ref, 1.0, stage_ref, sem)
        _convert_weight(bwin_hbm, bwin_ref, bwin_scale_ref, stage_ref, sem)
        _convert_weight(bwout_hbm, bwout_ref, 0.5, stage_ref, sem)
        x1a_ref[...] = jnp.zeros_like(x1a_ref)
        cc_ref[...] = jnp.zeros_like(cc_ref)
        hc_ref[...] = jnp.zeros_like(hc_ref)

    x1 = x1a_ref[...]
    x1b_ref[...] = x1
    hbb_ref[...] = _rms_norm(x1, nw_ref[1:2, :]).astype(jnp.bfloat16)

    first_tile = lax.rem(i - 1, tiles_per_row) == 0
    _interleave(
        _layer0(x_ref, nw_ref, awin_ref, lnw_ref, lnb_ref, wbd_ref, bsx_ref, awout_ref,
                hba_ref, vbuf_ref, yba_ref, x1a_ref),
        _layer1(first_tile, hbb_ref, x1b_ref, nw_ref, bwin_ref, convw_ref, convb_ref, wga_ref, wgx_ref, gab_ref,
                gxb_ref, lam_ref, bwout_ref, out_ref, ybb_ref, pin_ref, pout_ref, cc_ref, hc_ref))


def _const_spec(shape):
    zeros = (0,) * len(shape)
    return pl.BlockSpec(shape, lambda i: zeros, pipeline_mode=pl.Buffered(1))


def _pack_row_pairs(w):
    bits = lax.bitcast_convert_type(w.astype(jnp.bfloat16), jnp.uint16).astype(jnp.uint32)
    return bits[..., 0::2, :] | (bits[..., 1::2, :] << 16)


@jax.jit
def kernel(x, norm_w, a_w_in, a_ln_w, a_ln_b, a_w_s, a_b_s, a_w_out, b_w_in, b_conv_w, b_conv_b, b_gate_a_w, b_gate_a_b, b_gate_x_w, b_gate_x_b, b_lambda, b_w_out, norm_f_w):
    bsz, seq, d = x.shape
    assert d == D_MODEL and seq % TILE == 0 and TILE % CHUNK == 0
    assert norm_w.shape[0] == 2 and a_w_in.shape[0] == 1 and b_w_in.shape[0] == 1
    n_rep = TILE // CHUNK
    tiles_per_row = seq // TILE
    n_tiles = bsz * tiles_per_row
    f32, bf = jnp.float32, jnp.bfloat16

    nw = jnp.concatenate([norm_w, norm_f_w[None, :]], axis=0)
    awin_scale = jnp.concatenate([jnp.ones((2 * A_WIDTH,), f32), jnp.full((A_WIDTH,), 0.5, f32)]).reshape(-1, WCOLS)
    bwin_scale = jnp.concatenate([jnp.ones((B_WIDTH,), f32), jnp.full((B_WIDTH,), 0.5, f32)]).reshape(-1, WCOLS)
    causal = jnp.tril(jnp.ones((CHUNK, CHUNK), a_w_s.dtype))
    w_causal = 0.5 * a_w_s[0] * causal[None]
    eye = jnp.eye(n_rep, dtype=w_causal.dtype)
    wbd = _pack_row_pairs(jnp.einsum('mn,gts->gmtns', eye, w_causal).reshape(A_GROUPS, TILE, TILE))
    bsx = jnp.tile(jnp.repeat(0.5 * a_b_s[0].T, A_GROUP_DIM, axis=1), (n_rep, 1))

    def pair_block_diag(w):
        w = w.reshape(B_PAIRS, 2, B_HEAD_DIM, B_HEAD_DIM)
        z = jnp.zeros_like(w[:, 0])
        top = jnp.concatenate([w[:, 0], z], axis=2)
        bot = jnp.concatenate([z, w[:, 1]], axis=2)
        return jnp.concatenate([top, bot], axis=1)

    wga = _pack_row_pairs(pair_block_diag(0.5 * b_gate_a_w[0]))
    wgx = _pack_row_pairs(pair_block_diag(0.5 * b_gate_x_w[0]))
    row = lambda v: v.reshape(1, -1)

    def tile_index(t):
        return (t // tiles_per_row, t % tiles_per_row, 0)

    hbm = pl.BlockSpec(memory_space=pl.ANY)
    operands = [
        (x, pl.BlockSpec((None, TILE, D_MODEL), lambda i: tile_index(jnp.minimum(i, n_tiles - 1)))),
        (nw, None), (a_w_in, hbm), (awin_scale, None), (row(a_ln_w[0]), None), (row(a_ln_b[0]), None),
        (wbd, None), (bsx, None), (a_w_out, hbm),
        (b_w_in, hbm), (bwin_scale, None), (b_conv_w[0], None), (row(b_conv_b[0]), None), (wga, None), (wgx, None),
        (row(0.5 * b_gate_a_b[0]), None), (row(0.5 * b_gate_x_b[0]), None), (row(b_lambda[0]), None),
        (b_w_out, hbm),
    ]
    in_specs = [spec if spec is not None else _const_spec(op.shape) for op, spec in operands]

    scratch = [
        pltpu.VMEM((3 * A_WIDTH // WCOLS, D_MODEL, WCOLS), bf),
        pltpu.VMEM((1, A_WIDTH, D_MODEL), bf),
        pltpu.VMEM((2 * B_WIDTH // WCOLS, D_MODEL, WCOLS), bf),
        pltpu.VMEM((1, B_WIDTH, D_MODEL), bf),
        pltpu.VMEM((2, WROWS, WCOLS), f32),
        pltpu.SemaphoreType.DMA((2,)),
        pltpu.VMEM((TILE, D_MODEL), bf),
        pltpu.VMEM((TILE, A_WIDTH), f32),
        pltpu.VMEM((TILE, A_WIDTH), bf),
        pltpu.VMEM((TILE, D_MODEL), f32),
        pltpu.VMEM((TILE, D_MODEL), bf),
        pltpu.VMEM((TILE, D_MODEL), f32),
        pltpu.VMEM((TILE, B_WIDTH), bf),
        pltpu.VMEM((2, COLS // LANES, SUBLANES * PITCH, LANES), f32),
        pltpu.VMEM((2, COLS // LANES, SUBLANES * PITCH, LANES), f32),
        pltpu.VMEM((B_PAIRS, HIST, COLS), f32),
        pltpu.VMEM((B_PAIRS, SUBLANES, COLS), f32),
    ]
    return pl.pallas_call(
        functools.partial(_trunk_kernel, tiles_per_row),
        grid=(n_tiles + 1,),
        in_specs=in_specs,
        out_specs=pl.BlockSpec((None, TILE, D_MODEL), lambda i: tile_index(jnp.maximum(i - 1, 0))),
        out_shape=jax.ShapeDtypeStruct(x.shape, x.dtype),
        scratch_shapes=scratch,
        compiler_params=pltpu.CompilerParams(
            dimension_semantics=("arbitrary",),
            vmem_limit_bytes=VMEM_LIMIT_BYTES,
        ),
        name="hybrid_trunk",
    )(*[op for op, _ in operands])
```

```python
import functools
import math

import jax
import jax.numpy as jnp
from jax import lax
from jax.experimental import pallas as pl
from jax.experimental.pallas import tpu as pltpu

D_MODEL = 1024
RMS_EPS = 1e-6
LN_EPS = 1e-5
CHUNK = 128
A_WIDTH = 2 * D_MODEL
A_GROUPS = 8
A_GROUP_DIM = A_WIDTH // A_GROUPS
B_WIDTH = 3 * D_MODEL // 2
B_HEADS = 12
B_HEAD_DIM = B_WIDTH // B_HEADS
CONV_WIDTH = 4
RG_C = 8.0

SUBLANES = 8
LANES = 128
COLS = 256
B_PAIRS = B_WIDTH // COLS
TILE = 256
SEG = TILE // SUBLANES
PITCH = SEG + SUBLANES
HIST = (CONV_WIDTH - 1) * SUBLANES
WCOLS = 1024
WROWS = 256
W_STAGES = 6
VMEM_LIMIT_BYTES = 56 * 1024 * 1024

_GELU_C = math.sqrt(2.0 / math.pi)


def _gelu_x2(x):
    return x * (1.0 + jnp.tanh(x * (_GELU_C + (_GELU_C * 0.044715) * (x * x))))


def _one_plus_tanh(h):
    return 1.0 + jnp.tanh(h)


def _rms_norm(x, w):
    ms = jnp.mean(x * x, axis=-1, keepdims=True)
    return x * lax.rsqrt(ms + RMS_EPS) * w


def _dot(a, b):
    return jnp.dot(a, b, preferred_element_type=jnp.float32)


def _wcols(ref, lo, hi):
    assert lo // WCOLS == (hi - 1) // WCOLS
    return ref[lo // WCOLS, :, lo % WCOLS:lo % WCOLS + (hi - lo)]


def _packed(ref, idx):
    return pltpu.bitcast(ref[idx], jnp.bfloat16)


def _convert_weight(w_hbm, dst_ref, scale, stage_ref, sem):
    n_slab, k, _ = dst_ref.shape
    n_rb = k // WROWS
    n = n_slab * n_rb

    n_stage = stage_ref.shape[0]
    ahead = n_stage - 1

    def copy(q):
        rb, c = q // n_slab, q % n_slab
        src = w_hbm.at[0, pl.ds(rb * WROWS, WROWS), pl.ds(pl.multiple_of(c * WCOLS, WCOLS), WCOLS)]
        return pltpu.make_async_copy(src, stage_ref.at[q % n_stage], sem.at[q % n_stage])

    for q in range(min(ahead, n)):
        copy(q).start()

    def body(q, _):
        copy(q).wait()

        @pl.when(q + ahead < n)
        def _():
            copy(q + ahead).start()

        rb, c = q // n_slab, q % n_slab
        factor = scale if isinstance(scale, float) else scale[pl.ds(c, 1), :]
        dst_ref[c, pl.ds(pl.multiple_of(rb * WROWS, WROWS), WROWS), :] = (
            stage_ref[q % n_stage] * factor).astype(jnp.bfloat16)
        return 0

    lax.fori_loop(0, n, body, 0)


def _to_segment_order(buf_ref, x):
    n_slab = x.shape[1] // LANES
    for c in range(n_slab):
        for s in range(SUBLANES):
            buf_ref[c, s * PITCH:s * PITCH + SEG, :] = x[s * SEG:(s + 1) * SEG, c * LANES:(c + 1) * LANES]
    slabs = [jnp.concatenate([buf_ref[c, pl.ds(k, SUBLANES, stride=PITCH), :] for k in range(SEG)], axis=0)
             for c in range(n_slab)]
    return jnp.concatenate(slabs, axis=1)


def _from_segment_order(buf_ref, steps):
    n_slab = steps[0].shape[1] // LANES
    for c in range(n_slab):
        for k in range(SEG):
            buf_ref[c, pl.ds(k, SUBLANES, stride=PITCH), :] = steps[k][:, c * LANES:(c + 1) * LANES]
    slabs = [jnp.concatenate([buf_ref[c, s * PITCH:s * PITCH + SEG, :] for s in range(SUBLANES)], axis=0)
             for c in range(n_slab)]
    return jnp.concatenate(slabs, axis=1)


def _sublane_scan(p, h, sub):
    for k in (1, 2, 4):
        keep = sub >= k
        p_sh = jnp.where(keep, pltpu.roll(p, k, axis=0), 1.0)
        h_sh = jnp.where(keep, pltpu.roll(h, k, axis=0), 0.0)
        h = p * h_sh + h
        p = p * p_sh
    return p, h


def _layer0(x_ref, nw_ref, awin_ref, lnw_ref, lnb_ref, wbd_ref, bsx_ref, awout_ref,
            hb_ref, vbuf_ref, ybuf_ref, x1_ref, hb_next_ref):
    t = x_ref.shape[0]
    hb_ref[...] = _rms_norm(x_ref[...], nw_ref[0:1, :]).astype(jnp.bfloat16)
    yield

    s1 = jnp.zeros((t, 1), jnp.float32)
    s2 = jnp.zeros((t, 1), jnp.float32)
    for g in range(A_GROUPS):
        lo, hi = g * COLS, (g + 1) * COLS
        v2 = _gelu_x2(_dot(hb_ref[...], _wcols(awin_ref, A_WIDTH + lo, A_WIDTH + hi)))
        vbuf_ref[:, lo:hi] = v2
        s1 = s1 + jnp.sum(v2, axis=-1, keepdims=True)
        s2 = s2 + jnp.sum(v2 * v2, axis=-1, keepdims=True)
        yield
    mu = s1 * (1.0 / A_WIDTH)
    var = s2 * (1.0 / A_WIDTH) - mu * mu
    rstd = lax.rsqrt(var + 4.0 * LN_EPS)
    mu_b = jnp.broadcast_to(mu, (t, COLS))
    rstd_b = jnp.broadcast_to(rstd, (t, COLS))

    for g in range(A_GROUPS):
        lo, hi = g * COLS, (g + 1) * COLS
        u2 = _gelu_x2(_dot(hb_ref[...], _wcols(awin_ref, lo, hi)))
        gh = _dot(hb_ref[...], _wcols(awin_ref, 2 * A_WIDTH + lo, 2 * A_WIDTH + hi))
        gate = gh * _one_plus_tanh(gh)
        vn = (vbuf_ref[:, lo:hi] - mu_b) * rstd_b * lnw_ref[:, lo:hi] + lnb_ref[:, lo:hi]
        s_half = _dot(_packed(wbd_ref, g), vn.astype(jnp.bfloat16)) + bsx_ref[:, lo:hi]
        ybuf_ref[:, lo:hi] = (u2 * s_half * gate).astype(jnp.bfloat16)
        yield

    for n in range(D_MODEL // COLS):
        lo, hi = n * COLS, (n + 1) * COLS
        x1_ref[:, lo:hi] = x_ref[:, lo:hi] + _dot(ybuf_ref[...], _wcols(awout_ref, lo, hi))
        yield
    hb_next_ref[...] = _rms_norm(x1_ref[...], nw_ref[1:2, :]).astype(jnp.bfloat16)
    yield


def _layer1(first_tile, hb_ref, x1_ref, nw_ref, bwin_ref, convw_ref, convb_ref, wga_ref, wgx_ref, gab_ref, gxb_ref,
            lam_ref, bwout_ref, out_ref, ybuf_ref, pin_ref, pout_ref, cc_ref, hc_ref):
    t = x1_ref.shape[0]
    lam = lam_ref[...]
    half_neg_c_sp = (-0.5 * RG_C) * (jnp.maximum(-lam, 0.0) + jnp.log1p(jnp.exp(-jnp.abs(lam))))
    sub = lax.broadcasted_iota(jnp.int32, (SUBLANES, COLS), 0)
    sub_hist = lax.broadcasted_iota(jnp.int32, (CONV_WIDTH - 1, SUBLANES, COLS), 1).reshape(HIST, COLS)
    roll_rows = lambda z: pltpu.roll(z.reshape(CONV_WIDTH - 1, SUBLANES, COLS), 1, axis=1).reshape(HIST, COLS)

    for p in range(B_PAIRS):
        lo, hi = p * COLS, (p + 1) * COLS
        xb = _dot(hb_ref[...], _wcols(bwin_ref, lo, hi))
        gbh = _dot(hb_ref[...], _wcols(bwin_ref, B_WIDTH + lo, B_WIDTH + hi))
        xp = _to_segment_order(pin_ref.at[p % 2], xb)
        tail = xp[t - HIST:, :]
        prev_tail = jnp.where(first_tile, 0.0, cc_ref[p])
        head = jnp.where(sub_hist == 0, roll_rows(prev_tail), roll_rows(tail))
        cc_ref[p] = tail
        ext = jnp.concatenate([head, xp], axis=0)
        xc = convb_ref[:, lo:hi] + convw_ref[3:4, lo:hi] * xp
        for k in range(CONV_WIDTH - 1):
            xc = xc + convw_ref[k:k + 1, lo:hi] * ext[k * SUBLANES:k * SUBLANES + t, :]
        xcb = xc.astype(jnp.bfloat16)
        yield
        r2 = _one_plus_tanh(_dot(xcb, _packed(wga_ref, p)) + gab_ref[:, lo:hi])
        i2 = _one_plus_tanh(_dot(xcb, _packed(wgx_ref, p)) + gxb_ref[:, lo:hi])
        a = jnp.exp(r2 * half_neg_c_sp[:, lo:hi])
        om = 1.0 - a * a
        mult = om * lax.rsqrt(jnp.maximum(om, 1e-30))
        b2 = mult * (i2 * xc)
        hs, ps = [], []
        h_loc = jnp.zeros((SUBLANES, COLS), jnp.float32)
        p_loc = jnp.ones((SUBLANES, COLS), jnp.float32)
        for k in range(SEG):
            a_k = a[k * SUBLANES:(k + 1) * SUBLANES, :]
            h_loc = a_k * h_loc + b2[k * SUBLANES:(k + 1) * SUBLANES, :]
            p_loc = a_k * p_loc
            hs.append(h_loc)
            ps.append(p_loc)
        carry = jnp.where(first_tile, 0.0, hc_ref[p])
        p_cum, h_cum = _sublane_scan(p_loc, h_loc, sub)
        seg_end = h_cum + p_cum * carry
        seg_in = jnp.where(sub == 0, carry, pltpu.roll(seg_end, 1, axis=0))
        hc_ref[p] = jnp.broadcast_to(seg_end[SUBLANES - 1:SUBLANES, :], (SUBLANES, COLS))
        h2 = _from_segment_order(pout_ref.at[p % 2], [hs[k] + ps[k] * seg_in for k in range(SEG)])
        ybuf_ref[:, lo:hi] = (h2 * (gbh * _one_plus_tanh(gbh))).astype(jnp.bfloat16)
        yield

    x2 = []
    for n in range(D_MODEL // COLS):
        lo, hi = n * COLS, (n + 1) * COLS
        x2.append(x1_ref[:, lo:hi] + _dot(ybuf_ref[...], _wcols(bwout_ref, lo, hi)))
        yield
    out_ref[...] = _rms_norm(jnp.concatenate(x2, axis=1), nw_ref[2:3, :])


def _interleave(gens, pattern):
    for key in pattern:
        next(gens[key], None)
    for gen in gens.values():
        for _ in gen:
            pass


def _trunk_kernel(tiles_per_row,
                  x_ref, nw_ref, awin_hbm, awin_scale_ref, lnw_ref, lnb_ref, wbd_ref, bsx_ref, awout_hbm,
                  bwin_hbm, bwin_scale_ref, convw_ref, convb_ref, wga_ref, wgx_ref, gab_ref, gxb_ref, lam_ref,
                  bwout_hbm,
                  out_ref,
                  awin_ref, awout_ref, bwin_ref, bwout_ref, stage_ref, sem,
                  hba_ref, vbuf_ref, yba_ref, x1a_ref,
                  hbb_ref, x1b_ref, ybb_ref, pin_ref, pout_ref, cc_ref, hc_ref):
    i = pl.program_id(0)

    @pl.when(i == 0)
    def _():
        _convert_weight(awin_hbm, awin_ref, awin_scale_ref, stage_ref, sem)
        _convert_weight(awout_hbm, awout_ref, 1.0, stage_ref, sem)
        _convert_weight(bwin_hbm, bwin_ref, bwin_scale_ref, stage_ref, sem)
        _convert_weight(bwout_hbm, bwout_ref, 0.5, stage_ref, sem)
        x1a_ref[...] = jnp.zeros_like(x1a_ref)
        hbb_ref[...] = jnp.zeros_like(hbb_ref)
        cc_ref[...] = jnp.zeros_like(cc_ref)
        hc_ref[...] = jnp.zeros_like(hc_ref)

    x1b_ref[...] = x1a_ref[...]

    first_tile = lax.rem(i - 1, tiles_per_row) == 0
    gens = {
        "a": _layer0(x_ref, nw_ref, awin_ref, lnw_ref, lnb_ref, wbd_ref, bsx_ref, awout_ref,
                     hba_ref, vbuf_ref, yba_ref, x1a_ref, hbb_ref),
        "b": _layer1(first_tile, hbb_ref, x1b_ref, nw_ref, bwin_ref, convw_ref, convb_ref, wga_ref, wgx_ref, gab_ref,
                     gxb_ref, lam_ref, bwout_ref, out_ref, ybb_ref, pin_ref, pout_ref, cc_ref, hc_ref),
    }
    _interleave(gens, "baa" * (2 * B_PAIRS))


def _const_spec(shape):
    zeros = (0,) * len(shape)
    return pl.BlockSpec(shape, lambda i: zeros, pipeline_mode=pl.Buffered(1))


def _pack_row_pairs(w):
    w = w.astype(jnp.bfloat16)
    *lead, k, n = w.shape
    return lax.bitcast_convert_type(jnp.swapaxes(w.reshape(*lead, k // 2, 2, n), -1, -2), jnp.uint32)


@jax.jit
def kernel(x, norm_w, a_w_in, a_ln_w, a_ln_b, a_w_s, a_b_s, a_w_out, b_w_in, b_conv_w, b_conv_b, b_gate_a_w, b_gate_a_b, b_gate_x_w, b_gate_x_b, b_lambda, b_w_out, norm_f_w):
    bsz, seq, d = x.shape
    assert d == D_MODEL and seq % TILE == 0 and TILE % CHUNK == 0
    assert norm_w.shape[0] == 2 and a_w_in.shape[0] == 1 and b_w_in.shape[0] == 1
    n_rep = TILE // CHUNK
    tiles_per_row = seq // TILE
    n_tiles = bsz * tiles_per_row
    f32, bf = jnp.float32, jnp.bfloat16

    nw = jnp.concatenate([norm_w, norm_f_w[None, :]], axis=0)
    awin_scale = jnp.concatenate([jnp.ones((2 * A_WIDTH,), f32), jnp.full((A_WIDTH,), 0.5, f32)]).reshape(-1, WCOLS)
    bwin_scale = jnp.concatenate([jnp.ones((B_WIDTH,), f32), jnp.full((B_WIDTH,), 0.5, f32)]).reshape(-1, WCOLS)
    causal = jnp.tril(jnp.ones((CHUNK, CHUNK), a_w_s.dtype))
    w_causal = 0.5 * a_w_s[0] * causal[None]
    eye = jnp.eye(n_rep, dtype=w_causal.dtype)
    wbd = _pack_row_pairs(jnp.einsum('mn,gts->gmtns', eye, w_causal).reshape(A_GROUPS, TILE, TILE))
    bsx = jnp.tile(jnp.repeat(0.5 * a_b_s[0].T, A_GROUP_DIM, axis=1), (n_rep, 1))

    def pair_block_diag(w):
        w = w.reshape(B_PAIRS, 2, B_HEAD_DIM, B_HEAD_DIM)
        z = jnp.zeros_like(w[:, 0])
        top = jnp.concatenate([w[:, 0], z], axis=2)
        bot = jnp.concatenate([z, w[:, 1]], axis=2)
        return jnp.concatenate([top, bot], axis=1)

    wga = _pack_row_pairs(pair_block_diag(0.5 * b_gate_a_w[0]))
    wgx = _pack_row_pairs(pair_block_diag(0.5 * b_gate_x_w[0]))
    row = lambda v: v.reshape(1, -1)

    def tile_index(t):
        return (t // tiles_per_row, t % tiles_per_row, 0)

    hbm = pl.BlockSpec(memory_space=pl.ANY)
    operands = [
        (x, pl.BlockSpec((None, TILE, D_MODEL), lambda i: tile_index(jnp.minimum(i, n_tiles - 1)))),
        (nw, None), (a_w_in, hbm), (awin_scale, None), (row(a_ln_w[0]), None), (row(a_ln_b[0]), None),
        (wbd, None), (bsx, None), (a_w_out, hbm),
        (b_w_in, hbm), (bwin_scale, None), (b_conv_w[0], None), (row(b_conv_b[0]), None), (wga, None), (wgx, None),
        (row(0.5 * b_gate_a_b[0]), None), (row(0.5 * b_gate_x_b[0]), None), (row(b_lambda[0]), None),
        (b_w_out, hbm),
    ]
    in_specs = [spec if spec is not None else _const_spec(op.shape) for op, spec in operands]

    scratch = [
        pltpu.VMEM((3 * A_WIDTH // WCOLS, D_MODEL, WCOLS), bf),
        pltpu.VMEM((1, A_WIDTH, D_MODEL), bf),
        pltpu.VMEM((2 * B_WIDTH // WCOLS, D_MODEL, WCOLS), bf),
        pltpu.VMEM((1, B_WIDTH, D_MODEL), bf),
        pltpu.VMEM((W_STAGES, WROWS, WCOLS), f32),
        pltpu.SemaphoreType.DMA((W_STAGES,)),
        pltpu.VMEM((TILE, D_MODEL), bf),
        pltpu.VMEM((TILE, A_WIDTH), f32),
        pltpu.VMEM((TILE, A_WIDTH), bf),
        pltpu.VMEM((TILE, D_MODEL), f32),
        pltpu.VMEM((TILE, D_MODEL), bf),
        pltpu.VMEM((TILE, D_MODEL), f32),
        pltpu.VMEM((TILE, B_WIDTH), bf),
        pltpu.VMEM((2, COLS // LANES, SUBLANES * PITCH, LANES), f32),
        pltpu.VMEM((2, COLS // LANES, SUBLANES * PITCH, LANES), f32),
        pltpu.VMEM((B_PAIRS, HIST, COLS), f32),
        pltpu.VMEM((B_PAIRS, SUBLANES, COLS), f32),
    ]
    return pl.pallas_call(
        functools.partial(_trunk_kernel, tiles_per_row),
        grid=(n_tiles + 1,),
        in_specs=in_specs,
        out_specs=pl.BlockSpec((None, TILE, D_MODEL), lambda i: tile_index(jnp.maximum(i - 1, 0))),
        out_shape=jax.ShapeDtypeStruct(x.shape, x.dtype),
        scratch_shapes=scratch,
        compiler_params=pltpu.CompilerParams(
            dimension_semantics=("arbitrary",),
            vmem_limit_bytes=VMEM_LIMIT_BYTES,
        ),
        name="hybrid_trunk",
    )(*[op for op, _ in operands])
```

```python
import functools
import math

import jax
import jax.numpy as jnp
from jax import lax
from jax.experimental import pallas as pl
from jax.experimental.pallas import tpu as pltpu

D_MODEL = 1024
RMS_EPS = 1e-6
LN_EPS = 1e-5
CHUNK = 128
A_WIDTH = 2 * D_MODEL
A_GROUPS = 8
A_GROUP_DIM = A_WIDTH // A_GROUPS
B_WIDTH = 3 * D_MODEL // 2
B_HEADS = 12
B_HEAD_DIM = B_WIDTH // B_HEADS
CONV_WIDTH = 4
RG_C = 8.0

SUBLANES = 8
LANES = 128
COLS = 256
B_PAIRS = B_WIDTH // COLS
TILE = 256
SGU_ROWS = 256
SEG = TILE // SUBLANES
PITCH = SEG + SUBLANES
HIST = (CONV_WIDTH - 1) * SUBLANES
WCOLS = 1024
WROWS = 256
W_STAGES = 6
VMEM_LIMIT_BYTES = 56 * 1024 * 1024

_GELU_C = math.sqrt(2.0 / math.pi)


def _gelu_x2(x):
    return x * (1.0 + jnp.tanh(x * (_GELU_C + (_GELU_C * 0.044715) * (x * x))))


def _one_plus_tanh(h):
    return 1.0 + jnp.tanh(h)


def _rms_norm(x, w):
    ms = jnp.mean(x * x, axis=-1, keepdims=True)
    return x * lax.rsqrt(ms + RMS_EPS) * w


def _dot(a, b):
    return jnp.dot(a, b, preferred_element_type=jnp.float32)


def _wcols(ref, lo, hi):
    assert lo // WCOLS == (hi - 1) // WCOLS
    return ref[lo // WCOLS, :, lo % WCOLS:lo % WCOLS + (hi - lo)]


def _packed(ref, idx):
    return pltpu.bitcast(ref[idx], jnp.bfloat16)


def _convert_weight(w_hbm, dst_ref, scale, stage_ref, sem):
    n_slab, k, _ = dst_ref.shape
    n_rb = k // WROWS
    n = n_slab * n_rb

    n_stage = stage_ref.shape[0]
    ahead = n_stage - 1

    def copy(q):
        rb, c = q // n_slab, q % n_slab
        src = w_hbm.at[0, pl.ds(rb * WROWS, WROWS), pl.ds(pl.multiple_of(c * WCOLS, WCOLS), WCOLS)]
        return pltpu.make_async_copy(src, stage_ref.at[q % n_stage], sem.at[q % n_stage])

    for q in range(min(ahead, n)):
        copy(q).start()

    def body(q, _):
        copy(q).wait()

        @pl.when(q + ahead < n)
        def _():
            copy(q + ahead).start()

        rb, c = q // n_slab, q % n_slab
        factor = scale if isinstance(scale, float) else scale[pl.ds(c, 1), :]
        dst_ref[c, pl.ds(pl.multiple_of(rb * WROWS, WROWS), WROWS), :] = (
            stage_ref[q % n_stage] * factor).astype(jnp.bfloat16)
        return 0

    lax.fori_loop(0, n, body, 0)


def _to_segment_order(buf_ref, x):
    n_slab = x.shape[1] // LANES
    for c in range(n_slab):
        for s in range(SUBLANES):
            buf_ref[c, s * PITCH:s * PITCH + SEG, :] = x[s * SEG:(s + 1) * SEG, c * LANES:(c + 1) * LANES]
    slabs = [jnp.concatenate([buf_ref[c, pl.ds(k, SUBLANES, stride=PITCH), :] for k in range(SEG)], axis=0)
             for c in range(n_slab)]
    return jnp.concatenate(slabs, axis=1)


def _from_segment_order(buf_ref, steps):
    n_slab = steps[0].shape[1] // LANES
    for c in range(n_slab):
        for k in range(SEG):
            buf_ref[c, pl.ds(k, SUBLANES, stride=PITCH), :] = steps[k][:, c * LANES:(c + 1) * LANES]
    slabs = [jnp.concatenate([buf_ref[c, s * PITCH:s * PITCH + SEG, :] for s in range(SUBLANES)], axis=0)
             for c in range(n_slab)]
    return jnp.concatenate(slabs, axis=1)


def _sublane_scan(p, h, sub):
    for k in (1, 2, 4):
        keep = sub >= k
        p_sh = jnp.where(keep, pltpu.roll(p, k, axis=0), 1.0)
        h_sh = jnp.where(keep, pltpu.roll(h, k, axis=0), 0.0)
        h = p * h_sh + h
        p = p * p_sh
    return p, h


def _layer0(x_ref, nw_ref, awin_ref, lnw_ref, lnb_ref, wbd_ref, bsx_ref, awout_ref,
            hb_ref, vbuf_ref, ybuf_ref, x1_ref, hb_next_ref):
    t = x_ref.shape[0]
    hb_ref[...] = _rms_norm(x_ref[...], nw_ref[0:1, :]).astype(jnp.bfloat16)
    yield

    s1 = jnp.zeros((t, 1), jnp.float32)
    s2 = jnp.zeros((t, 1), jnp.float32)
    for g in range(A_GROUPS):
        lo, hi = g * COLS, (g + 1) * COLS
        v2 = _gelu_x2(_dot(hb_ref[...], _wcols(awin_ref, A_WIDTH + lo, A_WIDTH + hi)))
        vbuf_ref[:, lo:hi] = v2
        s1 = s1 + jnp.sum(v2, axis=-1, keepdims=True)
        s2 = s2 + jnp.sum(v2 * v2, axis=-1, keepdims=True)
        yield
    mu = s1 * (1.0 / A_WIDTH)
    var = jnp.maximum(s2 * (1.0 / A_WIDTH) - mu * mu, 0.0)
    rstd = lax.rsqrt(var + 4.0 * LN_EPS)
    mu_b = jnp.broadcast_to(mu, (t, COLS))
    rstd_b = jnp.broadcast_to(rstd, (t, COLS))

    for g in range(A_GROUPS):
        lo, hi = g * COLS, (g + 1) * COLS
        u2 = _gelu_x2(_dot(hb_ref[...], _wcols(awin_ref, lo, hi)))
        gh = _dot(hb_ref[...], _wcols(awin_ref, 2 * A_WIDTH + lo, 2 * A_WIDTH + hi))
        gate = gh * _one_plus_tanh(gh)
        vn = (vbuf_ref[:, lo:hi] - mu_b) * rstd_b * lnw_ref[:, lo:hi] + lnb_ref[:, lo:hi]
        vnb = vn.astype(jnp.bfloat16)
        s_half = jnp.concatenate(
            [_dot(_packed(wbd_ref, g), vnb[r:r + SGU_ROWS, :]) + bsx_ref[:, lo:hi] for r in range(0, t, SGU_ROWS)],
            axis=0)
        ybuf_ref[:, lo:hi] = (u2 * s_half * gate).astype(jnp.bfloat16)
        yield

    for n in range(D_MODEL // COLS):
        lo, hi = n * COLS, (n + 1) * COLS
        x1_ref[:, lo:hi] = x_ref[:, lo:hi] + _dot(ybuf_ref[...], _wcols(awout_ref, lo, hi))
        yield
    hb_next_ref[...] = _rms_norm(x1_ref[...], nw_ref[1:2, :]).astype(jnp.bfloat16)
    yield


def _layer1(first_tile, hb_ref, x1_ref, nw_ref, bwin_ref, convw_ref, convb_ref, wga_ref, wgx_ref, gab_ref, gxb_ref,
            lam_ref, bwout_ref, out_ref, ybuf_ref, pin_ref, pout_ref, cc_ref, hc_ref):
    t = x1_ref.shape[0]
    lam = lam_ref[...]
    half_neg_c_sp = (-0.5 * RG_C) * (jnp.maximum(-lam, 0.0) + jnp.log1p(jnp.exp(-jnp.abs(lam))))
    sub = lax.broadcasted_iota(jnp.int32, (SUBLANES, COLS), 0)
    sub_hist = lax.broadcasted_iota(jnp.int32, (CONV_WIDTH - 1, SUBLANES, COLS), 1).reshape(HIST, COLS)
    roll_rows = lambda z: pltpu.roll(z.reshape(CONV_WIDTH - 1, SUBLANES, COLS), 1, axis=1).reshape(HIST, COLS)

    for p in range(B_PAIRS):
        lo, hi = p * COLS, (p + 1) * COLS
        xb = _dot(hb_ref[...], _wcols(bwin_ref, lo, hi))
        gbh = _dot(hb_ref[...], _wcols(bwin_ref, B_WIDTH + lo, B_WIDTH + hi))
        xp = _to_segment_order(pin_ref.at[p % 2], xb)
        tail = xp[t - HIST:, :]
        prev_tail = jnp.where(first_tile, 0.0, cc_ref[p])
        head = jnp.where(sub_hist == 0, roll_rows(prev_tail), roll_rows(tail))
        cc_ref[p] = tail
        ext = jnp.concatenate([head, xp], axis=0)
        xc = convb_ref[:, lo:hi] + convw_ref[3:4, lo:hi] * xp
        for k in range(CONV_WIDTH - 1):
            xc = xc + convw_ref[k:k + 1, lo:hi] * ext[k * SUBLANES:k * SUBLANES + t, :]
        xcb = xc.astype(jnp.bfloat16)
        yield
        r2 = _one_plus_tanh(_dot(xcb, _packed(wga_ref, p)) + gab_ref[:, lo:hi])
        i2 = _one_plus_tanh(_dot(xcb, _packed(wgx_ref, p)) + gxb_ref[:, lo:hi])
        a = jnp.exp(r2 * half_neg_c_sp[:, lo:hi])
        om = 1.0 - a * a
        mult = om * lax.rsqrt(jnp.maximum(om, 1e-30))
        b2 = mult * (i2 * xc)
        hs, ps = [], []
        h_loc = jnp.zeros((SUBLANES, COLS), jnp.float32)
        p_loc = jnp.ones((SUBLANES, COLS), jnp.float32)
        for k in range(SEG):
            a_k = a[k * SUBLANES:(k + 1) * SUBLANES, :]
            h_loc = a_k * h_loc + b2[k * SUBLANES:(k + 1) * SUBLANES, :]
            p_loc = a_k * p_loc
            hs.append(h_loc)
            ps.append(p_loc)
        carry = jnp.where(first_tile, 0.0, hc_ref[p])
        p_cum, h_cum = _sublane_scan(p_loc, h_loc, sub)
        seg_end = h_cum + p_cum * carry
        seg_in = jnp.where(sub == 0, carry, pltpu.roll(seg_end, 1, axis=0))
        hc_ref[p] = jnp.broadcast_to(seg_end[SUBLANES - 1:SUBLANES, :], (SUBLANES, COLS))
        h2 = _from_segment_order(pout_ref.at[p % 2], [hs[k] + ps[k] * seg_in for k in range(SEG)])
        ybuf_ref[:, lo:hi] = (h2 * (gbh * _one_plus_tanh(gbh))).astype(jnp.bfloat16)
        yield

    x2 = []
    for n in range(D_MODEL // COLS):
        lo, hi = n * COLS, (n + 1) * COLS
        x2.append(x1_ref[:, lo:hi] + _dot(ybuf_ref[...], _wcols(bwout_ref, lo, hi)))
        yield
    out_ref[...] = _rms_norm(jnp.concatenate(x2, axis=1), nw_ref[2:3, :])


def _interleave(gens, pattern):
    for key in pattern:
        next(gens[key], None)
    for gen in gens.values():
        for _ in gen:
            pass


def _trunk_kernel(tiles_per_row,
                  x_ref, nw_ref, awin_hbm, awin_scale_ref, lnw_ref, lnb_ref, wbd_ref, bsx_ref, awout_hbm,
                  bwin_hbm, bwin_scale_ref, convw_ref, convb_ref, wga_ref, wgx_ref, gab_ref, gxb_ref, lam_ref,
                  bwout_hbm,
                  out_ref,
                  awin_ref, awout_ref, bwin_ref, bwout_ref, stage_ref, sem,
                  hba_ref, vbuf_ref, yba_ref, x1_ref,
                  hbb_ref, ybb_ref, pin_ref, pout_ref, cc_ref, hc_ref):
    i = pl.program_id(0)

    @pl.when(i == 0)
    def _():
        _convert_weight(awin_hbm, awin_ref, awin_scale_ref, stage_ref, sem)
        _convert_weight(awout_hbm, awout_ref, 1.0, stage_ref, sem)
        _convert_weight(bwin_hbm, bwin_ref, bwin_scale_ref, stage_ref, sem)
        _convert_weight(bwout_hbm, bwout_ref, 0.5, stage_ref, sem)
        x1_ref[...] = jnp.zeros_like(x1_ref)
        hbb_ref[...] = jnp.zeros_like(hbb_ref)
        cc_ref[...] = jnp.zeros_like(cc_ref)
        hc_ref[...] = jnp.zeros_like(hc_ref)

    slot = lax.rem(i, 2)
    first_tile = lax.rem(i - 1, tiles_per_row) == 0
    gens = {
        "a": _layer0(x_ref, nw_ref, awin_ref, lnw_ref, lnb_ref, wbd_ref, bsx_ref, awout_ref,
                     hba_ref, vbuf_ref, yba_ref, x1_ref.at[slot], hbb_ref),
        "b": _layer1(first_tile, hbb_ref, x1_ref.at[1 - slot], nw_ref, bwin_ref, convw_ref, convb_ref, wga_ref, wgx_ref, gab_ref,
                     gxb_ref, lam_ref, bwout_ref, out_ref, ybb_ref, pin_ref, pout_ref, cc_ref, hc_ref),
    }
    _interleave(gens, "baa" * (2 * B_PAIRS))


def _const_spec(shape):
    zeros = (0,) * len(shape)
    return pl.BlockSpec(shape, lambda i: zeros, pipeline_mode=pl.Buffered(1))


def _pack_row_pairs(w):
    w = w.astype(jnp.bfloat16)
    *lead, k, n = w.shape
    return lax.bitcast_convert_type(jnp.swapaxes(w.reshape(*lead, k // 2, 2, n), -1, -2), jnp.uint32)


@jax.jit
def kernel(x, norm_w, a_w_in, a_ln_w, a_ln_b, a_w_s, a_b_s, a_w_out, b_w_in, b_conv_w, b_conv_b, b_gate_a_w, b_gate_a_b, b_gate_x_w, b_gate_x_b, b_lambda, b_w_out, norm_f_w):
    bsz, seq, d = x.shape
    assert d == D_MODEL and seq % TILE == 0 and TILE % SGU_ROWS == 0 and SGU_ROWS % CHUNK == 0
    assert norm_w.shape[0] == 2 and a_w_in.shape[0] == 1 and b_w_in.shape[0] == 1
    n_rep = SGU_ROWS // CHUNK
    tiles_per_row = seq // TILE
    n_tiles = bsz * tiles_per_row
    f32, bf = jnp.float32, jnp.bfloat16

    nw = jnp.concatenate([norm_w, norm_f_w[None, :]], axis=0)
    awin_scale = jnp.concatenate([jnp.ones((2 * A_WIDTH,), f32), jnp.full((A_WIDTH,), 0.5, f32)]).reshape(-1, WCOLS)
    bwin_scale = jnp.concatenate([jnp.ones((B_WIDTH,), f32), jnp.full((B_WIDTH,), 0.5, f32)]).reshape(-1, WCOLS)
    causal = jnp.tril(jnp.ones((CHUNK, CHUNK), a_w_s.dtype))
    w_causal = 0.5 * a_w_s[0] * causal[None]
    eye = jnp.eye(n_rep, dtype=w_causal.dtype)
    wbd = _pack_row_pairs(jnp.einsum('mn,gts->gmtns', eye, w_causal).reshape(A_GROUPS, SGU_ROWS, SGU_ROWS))
    bsx = jnp.tile(jnp.repeat(0.5 * a_b_s[0].T, A_GROUP_DIM, axis=1), (n_rep, 1))

    def pair_block_diag(w):
        w = w.reshape(B_PAIRS, 2, B_HEAD_DIM, B_HEAD_DIM)
        z = jnp.zeros_like(w[:, 0])
        top = jnp.concatenate([w[:, 0], z], axis=2)
        bot = jnp.concatenate([z, w[:, 1]], axis=2)
        return jnp.concatenate([top, bot], axis=1)

    wga = _pack_row_pairs(pair_block_diag(0.5 * b_gate_a_w[0]))
    wgx = _pack_row_pairs(pair_block_diag(0.5 * b_gate_x_w[0]))
    row = lambda v: v.reshape(1, -1)

    def tile_index(t):
        return (t // tiles_per_row, t % tiles_per_row, 0)

    hbm = pl.BlockSpec(memory_space=pl.ANY)
    operands = [
        (x, pl.BlockSpec((None, TILE, D_MODEL), lambda i: tile_index(jnp.minimum(i, n_tiles - 1)))),
        (nw, None), (a_w_in, hbm), (awin_scale, None), (row(a_ln_w[0]), None), (row(a_ln_b[0]), None),
        (wbd, None), (bsx, None), (a_w_out, hbm),
        (b_w_in, hbm), (bwin_scale, None), (b_conv_w[0], None), (row(b_conv_b[0]), None), (wga, None), (wgx, None),
        (row(0.5 * b_gate_a_b[0]), None), (row(0.5 * b_gate_x_b[0]), None), (row(b_lambda[0]), None),
        (b_w_out, hbm),
    ]
    in_specs = [spec if spec is not None else _const_spec(op.shape) for op, spec in operands]

    scratch = [
        pltpu.VMEM((3 * A_WIDTH // WCOLS, D_MODEL, WCOLS), bf),
        pltpu.VMEM((1, A_WIDTH, D_MODEL), bf),
        pltpu.VMEM((2 * B_WIDTH // WCOLS, D_MODEL, WCOLS), bf),
        pltpu.VMEM((1, B_WIDTH, D_MODEL), bf),
        pltpu.VMEM((W_STAGES, WROWS, WCOLS), f32),
        pltpu.SemaphoreType.DMA((W_STAGES,)),
        pltpu.VMEM((TILE, D_MODEL), bf),
        pltpu.VMEM((TILE, A_WIDTH), f32),
        pltpu.VMEM((TILE, A_WIDTH), bf),
        pltpu.VMEM((2, TILE, D_MODEL), f32),
        pltpu.VMEM((TILE, D_MODEL), bf),
        pltpu.VMEM((TILE, B_WIDTH), bf),
        pltpu.VMEM((2, COLS // LANES, SUBLANES * PITCH, LANES), f32),
        pltpu.VMEM((2, COLS // LANES, SUBLANES * PITCH, LANES), f32),
        pltpu.VMEM((B_PAIRS, HIST, COLS), f32),
        pltpu.VMEM((B_PAIRS, SUBLANES, COLS), f32),
    ]
    return pl.pallas_call(
        functools.partial(_trunk_kernel, tiles_per_row),
        grid=(n_tiles + 1,),
        in_specs=in_specs,
        out_specs=pl.BlockSpec((None, TILE, D_MODEL), lambda i: tile_index(jnp.maximum(i - 1, 0))),
        out_shape=jax.ShapeDtypeStruct(x.shape, x.dtype),
        scratch_shapes=scratch,
        compiler_params=pltpu.CompilerParams(
            dimension_semantics=("arbitrary",),
            vmem_limit_bytes=VMEM_LIMIT_BYTES,
        ),
        name="hybrid_trunk",
    )(*[op for op, _ in operands])
```
